```python
import jax, jax.numpy as jnp
from jax import lax
import numpy as np

D_MODEL = 1024
BATCH = 8
SEQ = 2048
DEPTH = 4
DEC_BATCH = 32
DEC_SEQ = 1
PAST_LEN = 8192
PAGE_SIZE = 128

HEAD_DIM = 64
MIX_WIDTH = D_MODEL
N_HEADS = MIX_WIDTH // HEAD_DIM
N_SB_HEADS = N_HEADS // 2
N_MB_HEADS = N_HEADS - N_SB_HEADS
SB_WIDTH = N_SB_HEADS * HEAD_DIM
MB_WIDTH = N_MB_HEADS * HEAD_DIM
D_FF = 256 * ((8 * D_MODEL // 3 + 255) // 256)
MOBA_BLOCK = 256
MOBA_TOPK = 3
Q_BLOCK = 128
N_SUBLAYERS = 3
DN_ALPHA = (2.0 * DEPTH) ** 0.25
DN_BETA = (8.0 * DEPTH) ** -0.25
FFN_RESIDUAL = 0.5
LN_EPS = 1e-5
RMS_EPS = 1e-6

kernel_name = "hymba_stickbreak_moba_macaron_deepnorm_adaln_step"


def alibi_slopes(n):
    return jnp.asarray(2.0 ** (-8.0 * np.arange(1, n + 1) / n), dtype=jnp.float32)


def layer_norm(x, g, b):
    xf = x.astype(jnp.float32)
    mu = jnp.mean(xf, axis=-1, keepdims=True)
    var = jnp.mean(jnp.square(xf - mu), axis=-1, keepdims=True)
    return ((xf - mu) * lax.rsqrt(var + LN_EPS) * g + b).astype(x.dtype)


def group_rms_norm(o, g):
    of = o.astype(jnp.float32)
    return (of * lax.rsqrt(jnp.mean(of * of, axis=-1, keepdims=True) + RMS_EPS) * g).astype(o.dtype)


def ada_modulation(c, w_mod, b_mod):
    m = jax.nn.silu(c) @ w_mod + b_mod
    return m.reshape(c.shape[0], N_SUBLAYERS, 3, D_MODEL)


def modulate(x, m):
    return x * (1 + m[:, 1][:, None, :]) + m[:, 0][:, None, :]


def post_norm_residual(x, f_out, gate, weight, g, b):
    return layer_norm(DN_ALPHA * x + weight * gate[:, None, :] * f_out, g, b)


def swiglu(h, w_gu, w_down):
    g, u = jnp.split(h @ w_gu, 2, axis=-1)
    return (jax.nn.silu(g) * u) @ w_down


def stick_breaking_attention(q, k, v, q_pos0):
    B, T, H, dh = q.shape
    L = k.shape[1]
    qb = Q_BLOCK if T % Q_BLOCK == 0 else T
    n_qb = T // qb
    inv_sqrt = dh ** -0.5
    key_pos = jnp.arange(L)
    q_blocks = q.reshape(B, n_qb, qb, H, dh).transpose(1, 0, 2, 3, 4)

    def one_block(args):
        qi, blk = args
        t_pos = q_pos0 + blk * qb + jnp.arange(qb)
        z = jnp.einsum('bthd,bshd->bhts', qi, k, preferred_element_type=jnp.float32) * inv_sqrt
        past = key_pos[None, :] < t_pos[:, None]
        log_keep = jnp.where(past, jax.nn.log_sigmoid(-z), 0.0)
        incl = lax.cumsum(log_keep, axis=3, reverse=True)
        suffix = jnp.concatenate([incl[..., 1:], jnp.zeros_like(incl[..., :1])], axis=-1)
        a = jnp.where(past, jnp.exp(jax.nn.log_sigmoid(z) + suffix), 0.0)
        return jnp.einsum('bhts,bshd->bthd', a.astype(v.dtype), v)

    out = lax.map(one_block, (q_blocks, jnp.arange(n_qb)))
    return out.transpose(1, 0, 2, 3, 4).reshape(B, T, H, dh)


def moba_attention(q, k, v, q_pos0, slopes):
    B, T, H, dh = q.shape
    L = k.shape[1]
    nb = -(-L // MOBA_BLOCK)
    pad = nb * MOBA_BLOCK - L
    kp = jnp.pad(k, ((0, 0), (0, pad), (0, 0), (0, 0)))
    vp = jnp.pad(v, ((0, 0), (0, pad), (0, 0), (0, 0)))
    kb = kp.reshape(B, nb, MOBA_BLOCK, H, dh).transpose(0, 3, 1, 2, 4)
    vb = vp.reshape(B, nb, MOBA_BLOCK, H, dh).transpose(0, 3, 1, 2, 4)
    k_mean = jnp.mean(kb.astype(jnp.float32), axis=3)
    k_sel = min(MOBA_TOPK, nb)
    qb = Q_BLOCK if T % Q_BLOCK == 0 else T
    n_qb = T // qb
    inv_sqrt = dh ** -0.5
    offs = jnp.arange(MOBA_BLOCK)
    h_ix = jnp.arange(H)[:, None, None]
    q_blocks = q.reshape(B * n_qb, qb, H, dh)
    items = jnp.arange(B * n_qb)

    def one_block(args):
        qi, item = args
        b = item // n_qb
        blk = item % n_qb
        start = q_pos0 + blk * qb
        t_pos = start + jnp.arange(qb)
        own = start // MOBA_BLOCK
        kb_b, vb_b, km_b = kb[b], vb[b], k_mean[b]
        qh = qi.transpose(1, 0, 2)
        gate = jnp.einsum('htd,hnd->htn', qh.astype(jnp.float32), km_b)
        gate = jnp.where(jnp.arange(nb) < own, gate, -jnp.inf)
        _, sel = lax.top_k(gate, k_sel)
        sel_ok = jnp.arange(k_sel) < own
        k_g = kb_b[h_ix, sel]
        v_g = vb_b[h_ix, sel]
        k_own = lax.dynamic_index_in_dim(kb_b, own, axis=1, keepdims=False)
        v_own = lax.dynamic_index_in_dim(vb_b, own, axis=1, keepdims=False)
        s_sel = jnp.einsum('htd,htrsd->htrs', qh, k_g, preferred_element_type=jnp.float32) * inv_sqrt
        s_own = jnp.einsum('htd,hsd->hts', qh, k_own, preferred_element_type=jnp.float32) * inv_sqrt
        dist_sel = (t_pos[None, :, None, None] - (sel[..., None] * MOBA_BLOCK + offs)).astype(jnp.float32)
        dist_own = (t_pos[:, None] - (own * MOBA_BLOCK + offs)[None, :]).astype(jnp.float32)
        s_sel = jnp.where(sel_ok[None, None, :, None],
                          s_sel - slopes[:, None, None, None] * dist_sel, -jnp.inf)
        s_own = jnp.where(dist_own[None] >= 0,
                          s_own - slopes[:, None, None] * dist_own[None], -jnp.inf)
        s = jnp.concatenate([s_sel.reshape(H, qb, k_sel * MOBA_BLOCK), s_own], axis=-1)
        p = jax.nn.softmax(s, axis=-1).astype(v.dtype)
        p_sel = p[..., :k_sel * MOBA_BLOCK].reshape(H, qb, k_sel, MOBA_BLOCK)
        p_own = p[..., k_sel * MOBA_BLOCK:]
        o = jnp.einsum('htrs,htrsd->htd', p_sel, v_g) + jnp.einsum('hts,hsd->htd', p_own, v_own)
        return o.transpose(1, 0, 2)

    out = lax.map(one_block, (q_blocks, items))
    return out.reshape(B, T, H, dh)


def gather_past(cache, layer, page_table, h0, h1):
    rows = cache[layer, page_table, :, h0:h1]
    db, n_pages, page = rows.shape[:3]
    return rows.reshape(db, n_pages * page, h1 - h0, rows.shape[-1])


def trunk_layer(x, mod, ln_g, ln_b, w_gu, w_dn, w_in, w_gn, w_out, past, q_pos0, slopes):
    B, T, _ = x.shape
    x = post_norm_residual(x, swiglu(modulate(x, mod[:, 0]), w_gu[0], w_dn[0]),
                           mod[:, 0, 2], FFN_RESIDUAL, ln_g[0], ln_b[0])
    h = modulate(x, mod[:, 1])
    qkv = (h @ w_in).reshape(B, T, 3, N_HEADS, HEAD_DIM)
    q, k_new, v_new = qkv[:, :, 0], qkv[:, :, 1], qkv[:, :, 2]
    k_sb, v_sb = k_new[:, :, :N_SB_HEADS], v_new[:, :, :N_SB_HEADS]
    k_mb, v_mb = k_new[:, :, N_SB_HEADS:], v_new[:, :, N_SB_HEADS:]
    if past is not None:
        pk_sb, pv_sb, pk_mb, pv_mb = past
        k_sb = jnp.concatenate([pk_sb, k_sb], axis=1)
        v_sb = jnp.concatenate([pv_sb, v_sb], axis=1)
        k_mb = jnp.concatenate([pk_mb, k_mb], axis=1)
        v_mb = jnp.concatenate([pv_mb, v_mb], axis=1)
    o_sb = stick_breaking_attention(q[:, :, :N_SB_HEADS], k_sb, v_sb, q_pos0)
    o_mb = moba_attention(q[:, :, N_SB_HEADS:], k_mb, v_mb, q_pos0, slopes)
    o_sb = group_rms_norm(o_sb.reshape(B, T, SB_WIDTH), w_gn[:SB_WIDTH])
    o_mb = group_rms_norm(o_mb.reshape(B, T, MB_WIDTH), w_gn[SB_WIDTH:])
    mix = jnp.concatenate([o_sb, o_mb], axis=-1) @ w_out
    x = post_norm_residual(x, mix, mod[:, 1, 2], 1.0, ln_g[1], ln_b[1])
    x = post_norm_residual(x, swiglu(modulate(x, mod[:, 2]), w_gu[1], w_dn[1]),
                           mod[:, 2, 2], FFN_RESIDUAL, ln_g[2], ln_b[2])
    return x, k_new, v_new


def setup_inputs(seed: int = 0) -> dict:
    key = jax.random.key(seed)
    ks = jax.random.split(key, 18)
    f32 = jnp.float32
    n_pages = PAST_LEN // PAGE_SIZE
    used = DEC_BATCH * n_pages
    n_phys = used + max(1, used // 4)
    x_prompt = jax.random.normal(ks[0], (BATCH, SEQ, D_MODEL), f32)
    x_sample = jax.random.normal(ks[1], (DEC_BATCH, DEC_SEQ, D_MODEL), f32)
    cache_k = jax.random.normal(ks[2], (DEPTH, n_phys, PAGE_SIZE, N_HEADS, HEAD_DIM), f32)
    cache_v = DN_BETA * jax.random.normal(ks[3], (DEPTH, n_phys, PAGE_SIZE, N_HEADS, HEAD_DIM), f32)
    page_table = jax.random.permutation(ks[4], n_phys)[:used].reshape(DEC_BATCH, n_pages).astype(jnp.int32)
    c_prompt = jax.random.normal(ks[5], (BATCH, D_MODEL), f32)
    c_sample = jax.random.normal(ks[6], (DEC_BATCH, D_MODEL), f32)
    w_mod = jax.random.normal(ks[7], (DEPTH, D_MODEL, N_SUBLAYERS * 3 * D_MODEL), f32) * D_MODEL ** -0.5
    b_mod = 0.02 * jax.random.normal(ks[8], (DEPTH, N_SUBLAYERS * 3 * D_MODEL), f32)
    ln_g = 1.0 + 0.05 * jax.random.normal(ks[9], (DEPTH, N_SUBLAYERS, D_MODEL), f32)
    ln_b = 0.02 * jax.random.normal(ks[10], (DEPTH, N_SUBLAYERS, D_MODEL), f32)
    w_ffn_gu = jax.random.normal(ks[11], (DEPTH, 2, D_MODEL, 2 * D_FF), f32) * D_MODEL ** -0.5
    w_ffn_down = jax.random.normal(ks[12], (DEPTH, 2, D_FF, D_MODEL), f32) * (D_FF ** -0.5 * DN_BETA)
    col_scale = jnp.concatenate([jnp.ones((2 * MIX_WIDTH,), f32), jnp.full((MIX_WIDTH,), DN_BETA, f32)])
    w_in = jax.random.normal(ks[13], (DEPTH, D_MODEL, 3 * MIX_WIDTH), f32) * D_MODEL ** -0.5 * col_scale
    w_group_norm = 1.0 + 0.05 * jax.random.normal(ks[14], (DEPTH, MIX_WIDTH), f32)
    w_out = jax.random.normal(ks[15], (DEPTH, MIX_WIDTH, D_MODEL), f32) * (MIX_WIDTH ** -0.5 * DN_BETA)
    return {"x_prompt": x_prompt, "x_sample": x_sample, "cache_k": cache_k, "cache_v": cache_v,
            "page_table": page_table, "c_prompt": c_prompt, "c_sample": c_sample,
            "w_mod": w_mod, "b_mod": b_mod, "ln_g": ln_g, "ln_b": ln_b,
            "w_ffn_gu": w_ffn_gu, "w_ffn_down": w_ffn_down, "w_in": w_in,
            "w_group_norm": w_group_norm, "w_out": w_out}


def reference(x_prompt, x_sample, cache_k, cache_v, page_table, c_prompt, c_sample,
              w_mod, b_mod, ln_g, ln_b, w_ffn_gu, w_ffn_down, w_in, w_group_norm, w_out):
    slopes = alibi_slopes(N_MB_HEADS)
    past_len = page_table.shape[1] * cache_k.shape[2]
    xp, xs = x_prompt, x_sample
    pk, pv, sk, sv = [], [], [], []
    for layer in range(DEPTH):
        lw = (ln_g[layer], ln_b[layer], w_ffn_gu[layer], w_ffn_down[layer],
              w_in[layer], w_group_norm[layer], w_out[layer])
        mod_p = ada_modulation(c_prompt, w_mod[layer], b_mod[layer])
        mod_s = ada_modulation(c_sample, w_mod[layer], b_mod[layer])
        xp, kp_new, vp_new = trunk_layer(xp, mod_p, *lw, None, 0, slopes)
        past = (gather_past(cache_k, layer, page_table, 0, N_SB_HEADS),
                gather_past(cache_v, layer, page_table, 0, N_SB_HEADS),
                gather_past(cache_k, layer, page_table, N_SB_HEADS, N_HEADS),
                gather_past(cache_v, layer, page_table, N_SB_HEADS, N_HEADS))
        xs, ks_new, vs_new = trunk_layer(xs, mod_s, *lw, past, past_len, slopes)
        pk.append(kp_new)
        pv.append(vp_new)
        sk.append(ks_new)
        sv.append(vs_new)
    return (xp, xs, jnp.stack(pk), jnp.stack(pv), jnp.stack(sk), jnp.stack(sv))
```

```python
import functools

import jax
import jax.numpy as jnp
import numpy as np
from jax import lax
from jax.experimental import pallas as pl
from jax.experimental.pallas import tpu as pltpu

N_SUBLAYERS = 3
FFN_RESIDUAL = 0.5
LN_EPS = 1e-5
RMS_EPS = 1e-6
MOBA_BLOCK = 256
MOBA_TOPK = 3
LANES = 128
NEG_BIG = -1e30
VMEM_LIMIT = 56 * 1024 * 1024

F32 = jnp.float32
BF16 = jnp.bfloat16
HIGHEST = lax.Precision.HIGHEST


def _params(n_grid_dims):
    return pltpu.CompilerParams(
        dimension_semantics=("arbitrary",) * n_grid_dims,
        vmem_limit_bytes=VMEM_LIMIT)


def _resident(block_shape, index_map):
    return pl.BlockSpec(block_shape, index_map, pipeline_mode=pl.Buffered(1))


def _dot_t(a, b, precision=None):
    return lax.dot_general(a, b, (((1,), (1,)), ((), ())),
                           preferred_element_type=F32, precision=precision)


def _layer_norm(y, g, b):
    mu = jnp.mean(y, axis=-1, keepdims=True)
    d = y - mu
    var = jnp.mean(d * d, axis=-1, keepdims=True)
    return d * lax.rsqrt(var + LN_EPS) * g + b


def _softplus(z):
    return jnp.maximum(z, 0.0) + jnp.log1p(jnp.exp(-jnp.abs(z)))


def _top_blocks(gate, idx_f, n_idx, n_valid, axis):
    selected = jnp.zeros(gate.shape, F32)
    g = gate
    for r in range(min(MOBA_TOPK, n_idx)):
        mx = jnp.max(g, axis=axis, keepdims=True)
        idx = jnp.min(jnp.where(g == mx, idx_f, float(n_idx)), axis=axis, keepdims=True)
        hit = idx_f == idx
        selected = jnp.where(jnp.logical_and(hit, r < n_valid), 1.0, selected)
        g = jnp.where(hit, -jnp.inf, g)
    return selected


def _mod_kernel(c_ref, w_ref, b_ref, o_ref):
    c = c_ref[...]
    s = c * jax.nn.sigmoid(c)
    o_ref[...] = jnp.dot(s, w_ref[...], preferred_element_type=F32, precision=HIGHEST) + b_ref[...]


def _modulation(c_all, w_mod, b_mod):
    depth, d, n = w_mod.shape
    rows = c_all.shape[0]
    tn = n // 8
    return pl.pallas_call(
        _mod_kernel,
        grid=(depth, n // tn),
        in_specs=[pl.BlockSpec((rows, d), lambda l, j: (0, 0)),
                  pl.BlockSpec((None, d, tn), lambda l, j: (l, 0, j)),
                  pl.BlockSpec((None, 1, tn), lambda l, j: (l, 0, j))],
        out_specs=pl.BlockSpec((None, rows, tn), lambda l, j: (l, 0, j)),
        out_shape=jax.ShapeDtypeStruct((depth, rows, n), F32),
        compiler_params=_params(2),
        name="ada_modulation",
    )(c_all, w_mod, b_mod.reshape(depth, 1, n))


def _ffn_kernel(x_ref, sh_ref, sc_ref, gt_ref, wgu_ref, wd_ref, lg_ref, lb_ref, o_ref, *,
                d_ff, chunk, alpha):
    x = x_ref[...]
    h = (x * (1.0 + sc_ref[...]) + sh_ref[...]).astype(BF16)
    acc = jnp.zeros(x.shape, F32)
    for c0 in range(0, d_ff, chunk):
        g = jnp.dot(h, wgu_ref[:, c0:c0 + chunk], preferred_element_type=F32)
        u = jnp.dot(h, wgu_ref[:, d_ff + c0:d_ff + c0 + chunk], preferred_element_type=F32)
        act = (g * jax.nn.sigmoid(g) * u).astype(BF16)
        acc = acc + jnp.dot(act, wd_ref[c0:c0 + chunk, :], preferred_element_type=F32)
    y = alpha * x + FFN_RESIDUAL * gt_ref[...] * acc
    o_ref[...] = _layer_norm(y, lg_ref[...], lb_ref[...])


def _mod_spec(mod_rows, tm, tiles_per_seq, d):
    if mod_rows == 1:
        return pl.BlockSpec((None, 1, d), lambda i: (i // tiles_per_seq, 0, 0))
    return pl.BlockSpec((None, tm, d), lambda i: (i, 0, 0))


def _ffn(x, shift, scale, gate, w_gu, w_down, ln_g, ln_b, *, tm, tiles_per_seq, alpha):
    n_tok, d = x.shape
    d_ff = w_down.shape[0]
    chunk = d_ff // 2
    mspec = _mod_spec(shift.shape[1], tm, tiles_per_seq, d)
    row = pl.BlockSpec((tm, d), lambda i: (i, 0))
    return pl.pallas_call(
        functools.partial(_ffn_kernel, d_ff=d_ff, chunk=chunk, alpha=alpha),
        grid=(n_tok // tm,),
        in_specs=[row, mspec, mspec, mspec,
                  _resident((d, 2 * d_ff), lambda i: (0, 0)),
                  _resident((d_ff, d), lambda i: (0, 0)),
                  _resident((1, d), lambda i: (0, 0)),
                  _resident((1, d), lambda i: (0, 0))],
        out_specs=row,
        out_shape=jax.ShapeDtypeStruct((n_tok, d), F32),
        compiler_params=_params(1),
        name="ffn_sublayer",
    )(x, shift, scale, gate, w_gu, w_down, ln_g, ln_b)


def _qkv_sample_kernel(x_ref, sh_ref, sc_ref, w_ref, q_ref, k_ref, v_ref, *, mix):
    h = (x_ref[...] * (1.0 + sc_ref[...]) + sh_ref[...]).astype(BF16)
    qkv = jnp.dot(h, w_ref[...], preferred_element_type=F32)
    q_ref[...] = qkv[:, :mix]
    k_ref[...] = qkv[:, mix:2 * mix]
    v_ref[...] = qkv[:, 2 * mix:]


def _qkv_sample(x, shift, scale, w_in):
    n_tok, d = x.shape
    mix = w_in.shape[1] // 3
    full = lambda shape: pl.BlockSpec(shape, lambda i: (0,) * len(shape))
    mspec = _mod_spec(n_tok, n_tok, 1, d)
    return pl.pallas_call(
        functools.partial(_qkv_sample_kernel, mix=mix),
        grid=(1,),
        in_specs=[full((n_tok, d)), mspec, mspec, full((d, 3 * mix))],
        out_specs=[full((n_tok, mix))] * 3,
        out_shape=[jax.ShapeDtypeStruct((n_tok, mix), F32)] * 3,
        compiler_params=_params(1),
        name="qkv_projection_sample",
    )(x, shift, scale, w_in)


def _qkv_prompt_kernel(x_ref, sh_ref, sc_ref, wq_ref, wkvt_ref, *refs, mix, sb_width, inv_sqrt, n_alias):
    q32_ref, qb_ref, kt_ref, vt_ref, ktb_ref, vtb_ref, km_ref = refs[n_alias:]
    h = (x_ref[...] * (1.0 + sc_ref[...]) + sh_ref[...]).astype(BF16)
    q = jnp.dot(h, wq_ref[...], preferred_element_type=F32)
    q32_ref[...] = q[:, sb_width:]
    qb_ref[...] = (q * inv_sqrt).astype(BF16)
    kvt = _dot_t(wkvt_ref[...], h)
    kt, vt = kvt[:mix], kvt[mix:]
    kt_ref[...] = kt
    vt_ref[...] = vt
    for j in range(kt.shape[1] // MOBA_BLOCK):
        kj = kt[:, j * MOBA_BLOCK:(j + 1) * MOBA_BLOCK]
        ktb_ref[j] = kj.astype(BF16)
        vtb_ref[j] = vt[:, j * MOBA_BLOCK:(j + 1) * MOBA_BLOCK].astype(BF16)
        km_ref[:, j:j + 1] = jnp.sum(kj, axis=1, keepdims=True) * (1.0 / MOBA_BLOCK)


def _qkv_prompt(x, shift, scale, w_q, w_kv_t, kt_all, vt_all, *, layer, depth, batch, seq, tm, head_dim,
                sb_width):
    n_tok, d = x.shape
    mix = w_q.shape[1]
    tiles_per_seq = seq // tm
    blk_per_tile = tm // MOBA_BLOCK
    n_kb = seq // MOBA_BLOCK
    n_alias = 0 if kt_all is None else 2
    mspec = _mod_spec(1, tm, tiles_per_seq, d)
    row = pl.BlockSpec((tm, d), lambda i: (i, 0))
    stacked = pl.BlockSpec((None, None, mix, tm), lambda i: (layer, i // tiles_per_seq, 0, i % tiles_per_seq))
    blocked = pl.BlockSpec((None, blk_per_tile, mix, MOBA_BLOCK),
                           lambda i: (i // tiles_per_seq, i % tiles_per_seq, 0, 0))
    stacked_shape = jax.ShapeDtypeStruct((depth, batch, mix, seq), F32)
    blocked_shape = jax.ShapeDtypeStruct((batch, n_kb, mix, MOBA_BLOCK), BF16)
    in_specs = [row, mspec, mspec, _resident((d, mix), lambda i: (0, 0)),
                _resident((2 * mix, d), lambda i: (0, 0))]
    args = [x, shift, scale, w_q, w_kv_t]
    if n_alias:
        in_specs += [pl.BlockSpec(memory_space=pl.ANY)] * 2
        args += [kt_all, vt_all]
    return pl.pallas_call(
        functools.partial(_qkv_prompt_kernel, mix=mix, sb_width=sb_width, inv_sqrt=head_dim ** -0.5,
                          n_alias=n_alias),
        grid=(n_tok // tm,),
        in_specs=in_specs,
        out_specs=[pl.BlockSpec((tm, mix - sb_width), lambda i: (i, 0)),
                   pl.BlockSpec((tm, mix), lambda i: (i, 0)),
                   stacked, stacked, blocked, blocked,
                   pl.BlockSpec((None, mix, blk_per_tile), lambda i: (i, 0, 0))],
        out_shape=[jax.ShapeDtypeStruct((n_tok, mix - sb_width), F32),
                   jax.ShapeDtypeStruct((n_tok, mix), BF16),
                   stacked_shape, stacked_shape, blocked_shape, blocked_shape,
                   jax.ShapeDtypeStruct((n_tok // tm, mix, blk_per_tile), F32)],
        input_output_aliases={5: 2, 6: 3} if n_alias else {},
        compiler_params=_params(1),
        name="qkv_projection_prompt",
    )(*args)


def _sb_prompt_kernel(q_ref, k_ref, v_ref, tri_ref, o_ref, *, tq, head_dim):
    qi = pl.program_id(2)
    q = q_ref[...]
    tri = tri_ref[...]
    lane_head = lax.broadcasted_iota(jnp.int32, (1, LANES), 1) // head_dim
    row = lax.broadcasted_iota(jnp.int32, (tq, tq), 0)
    col = lax.broadcasted_iota(jnp.int32, (tq, tq), 1)
    past = col < row

    def key_block(qh, kj, carry, diagonal):
        c, acc = carry
        z = jnp.dot(qh, k_ref[kj], preferred_element_type=F32)
        sp = _softplus(z)
        log_keep = -sp
        log_beta = z - sp
        if diagonal:
            log_keep = jnp.where(past, log_keep, 0.0)
        suffix = jnp.dot(log_keep.astype(BF16), tri, preferred_element_type=F32) + c
        a = jnp.exp(log_beta + suffix)
        if diagonal:
            a = jnp.where(past, a, 0.0)
        acc = acc + _dot_t(a.astype(BF16), v_ref[kj])
        c = c + jnp.sum(log_keep, axis=1, keepdims=True)
        return c, acc

    outs = []
    for hh in range(LANES // head_dim):
        qh = jnp.where(lane_head == hh, q, jnp.zeros_like(q))
        carry = (jnp.zeros((tq, 1), F32), jnp.zeros((tq, LANES), F32))
        carry = key_block(qh, qi, carry, True)
        carry = lax.fori_loop(
            0, qi, lambda i, cr, qh=qh: key_block(qh, qi - 1 - i, cr, False), carry)
        outs.append(carry[1])
    o = outs[0]
    for hh in range(1, len(outs)):
        o = jnp.where(lane_head == hh, outs[hh], o)
    o_ref[...] = o


def _kv_block_spec(n_kb, col0):
    return pl.BlockSpec((None, n_kb, LANES, MOBA_BLOCK), lambda b, c, i: (b, 0, col0 + c, 0))


def _sb_prompt(qb, ktb, vtb, tri, *, n_cols, head_dim):
    batch, n_kb = ktb.shape[:2]
    tq = MOBA_BLOCK
    return pl.pallas_call(
        functools.partial(_sb_prompt_kernel, tq=tq, head_dim=head_dim),
        grid=(batch, n_cols, n_kb),
        in_specs=[pl.BlockSpec((tq, LANES), lambda b, c, i: (b * n_kb + i, c)),
                  _kv_block_spec(n_kb, 0), _kv_block_spec(n_kb, 0),
                  _resident((tq, tq), lambda b, c, i: (0, 0))],
        out_specs=pl.BlockSpec((tq, LANES), lambda b, c, i: (b * n_kb + i, c)),
        out_shape=jax.ShapeDtypeStruct((batch * n_kb * tq, n_cols * LANES), F32),
        compiler_params=_params(3),
        name="sb_attention_prompt",
    )(qb, ktb, vtb, tri)


def _moba_prompt_kernel(slopes_ref, q_ref, q32_ref, k_ref, v_ref, km_ref, o_ref, *, tq, head_dim, n_blocks):
    cb = pl.program_id(1)
    qi = pl.program_id(2)
    heads_per_col = LANES // head_dim
    q = q_ref[...]
    q32 = q32_ref[...]
    km = km_ref[...]
    lane_head = lax.broadcasted_iota(jnp.int32, (1, LANES), 1) // head_dim
    feat_head = lax.broadcasted_iota(jnp.int32, (LANES, 1), 0) // head_dim
    blk = lax.broadcasted_iota(jnp.int32, (tq, n_blocks), 1)
    row = lax.broadcasted_iota(jnp.int32, (tq, tq), 0)
    col = lax.broadcasted_iota(jnp.int32, (tq, tq), 1)
    rel = (row - col).astype(F32)

    def key_block(qh, slope, selected, kj, carry, own):
        m, l, acc = carry
        s = jnp.dot(qh, k_ref[kj], preferred_element_type=F32)
        dist = rel + ((qi - kj) * tq).astype(F32)
        s = s - slope * dist
        if own:
            s = jnp.where(rel >= 0.0, s, NEG_BIG)
        else:
            picked = jnp.max(jnp.where(blk == kj, selected, 0.0), axis=1, keepdims=True)
            s = jnp.where(picked > 0.0, s, NEG_BIG)
        m_new = jnp.maximum(m, jnp.max(s, axis=1, keepdims=True))
        scale = jnp.exp(m - m_new)
        p = jnp.exp(s - m_new)
        l = scale * l + jnp.sum(p, axis=1, keepdims=True)
        acc = scale * acc + _dot_t(p.astype(BF16), v_ref[kj])
        return m_new, l, acc

    outs = []
    for hh in range(heads_per_col):
        qh = jnp.where(lane_head == hh, q, jnp.zeros_like(q))
        slope = slopes_ref[cb * heads_per_col + hh]
        gate = jnp.dot(jnp.where(lane_head == hh, q32, 0.0), jnp.where(feat_head == hh, km, 0.0),
                       preferred_element_type=F32, precision=HIGHEST)
        gate = jnp.where(blk < qi, gate, -jnp.inf)
        selected = _top_blocks(gate, blk.astype(F32), n_blocks, qi, axis=1)
        carry = (jnp.full((tq, 1), NEG_BIG, F32), jnp.zeros((tq, 1), F32), jnp.zeros((tq, LANES), F32))
        carry = key_block(qh, slope, selected, qi, carry, True)
        carry = lax.fori_loop(
            0, qi,
            lambda j, cr, qh=qh, slope=slope, selected=selected: key_block(qh, slope, selected, j, cr, False),
            carry)
        outs.append(carry[2] / carry[1])
    o = outs[0]
    for hh in range(1, len(outs)):
        o = jnp.where(lane_head == hh, outs[hh], o)
    o_ref[...] = o


def _moba_prompt(slopes, qb, q32, ktb, vtb, kmean_t, *, col0, n_cols, head_dim):
    batch, n_kb = ktb.shape[:2]
    tq = MOBA_BLOCK
    return pl.pallas_call(
        functools.partial(_moba_prompt_kernel, tq=tq, head_dim=head_dim, n_blocks=n_kb),
        grid=(batch, n_cols, n_kb),
        in_specs=[pl.BlockSpec(memory_space=pltpu.SMEM),
                  pl.BlockSpec((tq, LANES), lambda b, c, i: (b * n_kb + i, col0 + c)),
                  pl.BlockSpec((tq, LANES), lambda b, c, i: (b * n_kb + i, c)),
                  _kv_block_spec(n_kb, col0), _kv_block_spec(n_kb, col0),
                  pl.BlockSpec((None, LANES, n_kb), lambda b, c, i: (b, col0 + c, 0))],
        out_specs=pl.BlockSpec((tq, LANES), lambda b, c, i: (b * n_kb + i, c)),
        out_shape=jax.ShapeDtypeStruct((batch * n_kb * tq, n_cols * LANES), F32),
        compiler_params=_params(3),
        name="moba_attention_prompt",
    )(slopes, qb, q32, ktb, vtb, kmean_t)


def _outproj_kernel(x_ref, osb_ref, omb_ref, gt_ref, gn_ref, w_ref, lg_ref, lb_ref, o_ref, *, alpha):
    x = x_ref[...]
    gn = gn_ref[...]
    w_sb = osb_ref.shape[1]

    def group_norm(o, g):
        return (o * lax.rsqrt(jnp.mean(o * o, axis=-1, keepdims=True) + RMS_EPS) * g).astype(BF16)

    n_sb = group_norm(osb_ref[...], gn[:, :w_sb])
    n_mb = group_norm(omb_ref[...], gn[:, w_sb:])
    mix = (jnp.dot(n_sb, w_ref[:w_sb, :], preferred_element_type=F32)
           + jnp.dot(n_mb, w_ref[w_sb:, :], preferred_element_type=F32))
    y = alpha * x + gt_ref[...] * mix
    o_ref[...] = _layer_norm(y, lg_ref[...], lb_ref[...])


def _outproj(x, o_sb, o_mb, gate, w_gn, w_out, ln_g, ln_b, *, tm, tiles_per_seq, alpha):
    n_tok, d = x.shape
    mspec = _mod_spec(gate.shape[1], tm, tiles_per_seq, d)
    row = pl.BlockSpec((tm, d), lambda i: (i, 0))
    return pl.pallas_call(
        functools.partial(_outproj_kernel, alpha=alpha),
        grid=(n_tok // tm,),
        in_specs=[row,
                  pl.BlockSpec((tm, o_sb.shape[1]), lambda i: (i, 0)),
                  pl.BlockSpec((tm, o_mb.shape[1]), lambda i: (i, 0)),
                  mspec,
                  _resident((1, w_gn.shape[1]), lambda i: (0, 0)),
                  _resident(w_out.shape, lambda i: (0, 0)),
                  _resident((1, d), lambda i: (0, 0)),
                  _resident((1, d), lambda i: (0, 0))],
        out_specs=row,
        out_shape=jax.ShapeDtypeStruct((n_tok, d), F32),
        compiler_params=_params(1),
        name="mix_outproj",
    )(x, o_sb, o_mb, gate, w_gn, w_out, ln_g, ln_b)


def _sample_attn_kernel(pt_ref, qc_ref, q_ref, kn_ref, vnt_ref, slope_ref, tri_ref, *refs,
                        n_blk, pages_per_blk, page, n_heads, n_sb, head_dim, past_len):
    del pt_ref
    k_refs = refs[:pages_per_blk]
    v_refs = refs[pages_per_blk:2 * pages_per_blk]
    o_ref = refs[2 * pages_per_blk]
    c_sc, osb_sc, gate_sc, m_sc, l_sc, omb_sc = refs[2 * pages_per_blk + 1:]

    j = pl.program_id(1)
    blk = n_blk - 1 - j
    n_mb = n_heads - n_sb
    inv_sqrt = head_dim ** -0.5

    @pl.when(j == 0)
    def _():
        c_sc[...] = jnp.zeros_like(c_sc)
        osb_sc[...] = jnp.zeros_like(osb_sc)
        gate_sc[...] = jnp.zeros_like(gate_sc)
        m_sc[...] = jnp.zeros_like(m_sc)
        l_sc[...] = jnp.zeros_like(l_sc)

    mb_r = lax.broadcasted_iota(jnp.int32, (n_mb, n_heads), 0)
    mb_l = lax.broadcasted_iota(jnp.int32, (n_mb, n_heads), 1)
    mb_slot = jnp.where(mb_l == mb_r + n_sb, 1.0, 0.0)

    def to_lanes(colv):
        return jnp.sum(mb_slot * colv, axis=0, keepdims=True)

    qc = qc_ref[...]
    zraw = jnp.concatenate([jnp.sum(k_refs[p][...] * qc, axis=1) for p in range(pages_per_blk)], axis=1)
    z = zraw * inv_sqrt

    z_sb = z[:n_sb]
    sp = _softplus(z_sb)
    log_keep = -sp
    suffix = jnp.dot(log_keep, tri_ref[...], preferred_element_type=F32, precision=HIGHEST) + c_sc[...]
    a_sb = jnp.exp(z_sb - sp + suffix)
    c_sc[...] = c_sc[...] + jnp.sum(log_keep, axis=1, keepdims=True)

    tok = lax.broadcasted_iota(jnp.int32, (1, MOBA_BLOCK), 1)
    dist = (past_len - blk * MOBA_BLOCK - tok).astype(F32)
    s = z[n_sb:] - slope_ref[...] * dist
    m_blk = jnp.max(s, axis=1, keepdims=True)
    p_mb = jnp.exp(s - m_blk)
    l_blk = jnp.sum(p_mb, axis=1, keepdims=True)
    g_blk = jnp.sum(zraw[n_sb:], axis=1, keepdims=True) * (1.0 / MOBA_BLOCK)
    here = lax.broadcasted_iota(jnp.int32, (n_blk, n_heads), 0) == blk
    m_sc[...] = jnp.where(here, to_lanes(m_blk), m_sc[...])
    l_sc[...] = jnp.where(here, to_lanes(l_blk), l_sc[...])
    gate_sc[...] = jnp.where(here, to_lanes(g_blk), gate_sc[...])

    w_all = jnp.concatenate([a_sb, p_mb], axis=0).astype(BF16)
    w_row = lax.broadcasted_iota(jnp.int32, (n_heads, MOBA_BLOCK), 0)
    acc = jnp.zeros((head_dim, n_heads), F32)
    for h in range(n_heads):
        vt_h = jnp.concatenate([v_refs[p][h] for p in range(pages_per_blk)], axis=1).astype(BF16)
        acc = acc + _dot_t(vt_h, jnp.where(w_row == h, w_all, jnp.zeros_like(w_all)))
    osb_sc[...] = osb_sc[...] + acc
    omb_sc[blk] = acc

    @pl.when(j == n_blk - 1)
    def _():
        blk_f = lax.broadcasted_iota(jnp.int32, (n_blk, n_heads), 0).astype(F32)
        selected = _top_blocks(gate_sc[...], blk_f, n_blk + 1, n_blk, axis=0)
        s_self = to_lanes(jnp.sum(q_ref[...] * kn_ref[...], axis=1, keepdims=True)[n_sb:]) * inv_sqrt
        m_fin = jnp.maximum(jnp.max(jnp.where(selected > 0.0, m_sc[...], NEG_BIG), axis=0, keepdims=True),
                            s_self)
        coef = jnp.where(selected > 0.0, jnp.exp(m_sc[...] - m_fin), 0.0)
        w_self = jnp.exp(s_self - m_fin)
        denom = jnp.sum(coef * l_sc[...], axis=0, keepdims=True) + w_self
        o_mb = w_self * vnt_ref[...]
        for b in range(n_blk):
            o_mb = o_mb + coef[b:b + 1, :] * omb_sc[b]
        head = lax.broadcasted_iota(jnp.int32, (1, n_heads), 1)
        o_ref[...] = jnp.where(head < n_sb, osb_sc[...], o_mb / denom)


def _sample_attention(layer, page_table, q_col, q, k_new, v_new_t, slope_col, tri, cache_kt, cache_vt, *, n_sb):
    db, n_heads, head_dim = q.shape
    page = cache_kt.shape[4]
    n_pages = page_table.shape[1]
    pages_per_blk = MOBA_BLOCK // page
    n_blk = n_pages // pages_per_blk
    past_len = n_pages * page
    n_mb = n_heads - n_sb

    def page_spec(p):
        return pl.BlockSpec(
            (None, None, n_heads, head_dim, page),
            lambda b, j, pt, p=p: (layer, pt[b * n_pages + (n_blk - 1 - j) * pages_per_blk + p], 0, 0, 0))

    per_seq = lambda *shape: pl.BlockSpec((None,) + shape, lambda b, j, pt: (b,) + (0,) * len(shape))
    grid_spec = pltpu.PrefetchScalarGridSpec(
        num_scalar_prefetch=1,
        grid=(db, n_blk),
        in_specs=[per_seq(n_heads, head_dim, 1), per_seq(n_heads, head_dim), per_seq(n_heads, head_dim),
                  per_seq(head_dim, n_heads),
                  pl.BlockSpec((n_mb, 1), lambda b, j, pt: (0, 0)),
                  pl.BlockSpec((MOBA_BLOCK, MOBA_BLOCK), lambda b, j, pt: (0, 0))]
                 + [page_spec(p) for p in range(pages_per_blk)] * 2,
        out_specs=per_seq(head_dim, n_heads),
        scratch_shapes=[pltpu.VMEM((n_sb, 1), F32),
                        pltpu.VMEM((head_dim, n_heads), F32),
                        pltpu.VMEM((n_blk, n_heads), F32),
                        pltpu.VMEM((n_blk, n_heads), F32),
                        pltpu.VMEM((n_blk, n_heads), F32),
                        pltpu.VMEM((n_blk, head_dim, n_heads), F32)])
    return pl.pallas_call(
        functools.partial(_sample_attn_kernel, n_blk=n_blk, pages_per_blk=pages_per_blk, page=page,
                          n_heads=n_heads, n_sb=n_sb, head_dim=head_dim, past_len=past_len),
        grid_spec=grid_spec,
        out_shape=jax.ShapeDtypeStruct((db, head_dim, n_heads), F32),
        compiler_params=_params(2),
        name="sample_paged_attention",
    )(page_table.reshape(-1), q_col, q, k_new, v_new_t, slope_col, tri,
      *([cache_kt] * pages_per_blk), *([cache_vt] * pages_per_blk))


def _alibi_slopes(n):
    return np.asarray(2.0 ** (-8.0 * np.arange(1, n + 1) / n), dtype=np.float32)


def kernel(x_prompt, x_sample, cache_k, cache_v, page_table, c_prompt, c_sample, w_mod, b_mod,
           ln_g, ln_b, w_ffn_gu, w_ffn_down, w_in, w_group_norm, w_out):
    batch, seq, d = x_prompt.shape
    db, dec_seq, _ = x_sample.shape
    depth, _, page, n_heads, head_dim = cache_k.shape
    mix = n_heads * head_dim
    n_sb = n_heads // 2
    n_mb = n_heads - n_sb
    sb_width = n_sb * head_dim
    alpha = (2.0 * depth) ** 0.25
    assert dec_seq == 1 and mix == d
    assert seq % MOBA_BLOCK == 0 and MOBA_BLOCK % page == 0
    assert (page_table.shape[1] * page) % MOBA_BLOCK == 0
    assert sb_width % LANES == 0 and LANES % head_dim == 0

    tm = 512 if seq % 512 == 0 else MOBA_BLOCK
    tiles_per_seq = seq // tm
    n_tok = batch * seq
    n_kb = seq // MOBA_BLOCK
    sb_cols = sb_width // LANES
    mb_cols = (mix - sb_width) // LANES

    slopes = jnp.asarray(_alibi_slopes(n_mb))
    tri_np = np.tril(np.ones((MOBA_BLOCK, MOBA_BLOCK), np.float32), -1)
    tri_f32 = jnp.asarray(tri_np)
    tri_bf16 = jnp.asarray(tri_np, dtype=BF16)

    w_gu_b = w_ffn_gu.astype(BF16)
    w_dn_b = w_ffn_down.astype(BF16)
    w_in_b = w_in.astype(BF16)
    w_q_b = w_in_b[:, :, :mix]
    w_kv_t_b = jnp.swapaxes(w_in_b[:, :, mix:], 1, 2)
    w_out_b = w_out.astype(BF16)

    mod = _modulation(jnp.concatenate([c_prompt, c_sample], axis=0), w_mod, b_mod)
    mod = mod.reshape(depth, batch + db, N_SUBLAYERS, 3, d)

    cache_kt = jnp.transpose(cache_k, (0, 1, 3, 4, 2))
    cache_vt = jnp.transpose(cache_v, (0, 1, 3, 4, 2))

    xp = x_prompt.reshape(n_tok, d)
    xs = x_sample.reshape(db, d)
    kt_all = vt_all = None
    sk, sv = [], []
    for layer in range(depth):
        def mods(sub, prompt):
            if prompt:
                m = mod[layer, :batch, sub]
                return tuple(m[:, i][:, None, :] for i in range(3))
            m = mod[layer, batch:, sub]
            return tuple(m[:, i][None, :, :] for i in range(3))

        def lnp(sub):
            return ln_g[layer, sub][None, :], ln_b[layer, sub][None, :]

        def ffn(x, sub, which, prompt):
            sh, sc, gt = mods(sub, prompt)
            g, b = lnp(sub)
            return _ffn(x, sh, sc, gt, w_gu_b[layer, which], w_dn_b[layer, which], g, b,
                        tm=tm if prompt else db, tiles_per_seq=tiles_per_seq, alpha=alpha)

        xp = ffn(xp, 0, 0, True)
        sh, sc, gt = mods(1, True)
        q32, qb, kt_all, vt_all, ktb, vtb, km = _qkv_prompt(
            xp, sh, sc, w_q_b[layer], w_kv_t_b[layer], kt_all, vt_all, layer=layer, depth=depth,
            batch=batch, seq=seq, tm=tm, head_dim=head_dim, sb_width=sb_width)
        kmean_t = jnp.transpose(km.reshape(batch, tiles_per_seq, mix, tm // MOBA_BLOCK),
                                (0, 2, 1, 3)).reshape(batch, mix, n_kb)
        o_sb = _sb_prompt(qb, ktb, vtb, tri_bf16, n_cols=sb_cols, head_dim=head_dim)
        o_mb = _moba_prompt(slopes, qb, q32, ktb, vtb, kmean_t, col0=sb_cols, n_cols=mb_cols,
                            head_dim=head_dim)
        g, b = lnp(1)
        xp = _outproj(xp, o_sb, o_mb, gt, w_group_norm[layer][None, :], w_out_b[layer], g, b,
                      tm=tm, tiles_per_seq=tiles_per_seq, alpha=alpha)
        xp = ffn(xp, 2, 1, True)

        xs = ffn(xs, 0, 0, False)
        sh, sc, gt = mods(1, False)
        q32, k32, v32 = _qkv_sample(xs, sh, sc, w_in_b[layer])
        q3 = q32.reshape(db, n_heads, head_dim)
        k3 = k32.reshape(db, n_heads, head_dim)
        v3 = v32.reshape(db, n_heads, head_dim)
        o_t = _sample_attention(layer, page_table, q3[..., None], q3, k3, jnp.swapaxes(v3, 1, 2),
                                slopes.reshape(n_mb, 1), tri_f32, cache_kt, cache_vt, n_sb=n_sb)
        o = jnp.swapaxes(o_t, 1, 2).reshape(db, mix)
        g, b = lnp(1)
        xs = _outproj(xs, o[:, :sb_width], o[:, sb_width:], gt, w_group_norm[layer][None, :], w_out_b[layer],
                      g, b, tm=db, tiles_per_seq=1, alpha=alpha)
        xs = ffn(xs, 2, 1, False)
        sk.append(k3.reshape(db, 1, n_heads, head_dim))
        sv.append(v3.reshape(db, 1, n_heads, head_dim))

    def token_major(t_all):
        return jnp.transpose(t_all.reshape(depth, batch, n_heads, head_dim, seq), (0, 1, 4, 2, 3))

    return (xp.reshape(batch, seq, d), xs.reshape(db, 1, d),
            token_major(kt_all), token_major(vt_all), jnp.stack(sk), jnp.stack(sv))
```

```python
import functools

import jax
import jax.numpy as jnp
import numpy as np
from jax import lax
from jax.experimental import pallas as pl
from jax.experimental.pallas import tpu as pltpu

N_SUBLAYERS = 3
FFN_RESIDUAL = 0.5
LN_EPS = 1e-5
RMS_EPS = 1e-6
MOBA_BLOCK = 256
MOBA_TOPK = 3
LANES = 128
NEG_BIG = -1e30
VMEM_LIMIT = 56 * 1024 * 1024

F32 = jnp.float32
BF16 = jnp.bfloat16
HIGHEST = lax.Precision.HIGHEST


def _params(n_grid_dims):
    return pltpu.CompilerParams(
        dimension_semantics=("arbitrary",) * n_grid_dims,
        vmem_limit_bytes=VMEM_LIMIT)


def _resident(block_shape, index_map):
    return pl.BlockSpec(block_shape, index_map, pipeline_mode=pl.Buffered(1))


def _dot_t(a, b, precision=None):
    return lax.dot_general(a, b, (((1,), (1,)), ((), ())),
                           preferred_element_type=F32, precision=precision)


def _layer_norm(y, g, b):
    mu = jnp.mean(y, axis=-1, keepdims=True)
    d = y - mu
    var = jnp.mean(d * d, axis=-1, keepdims=True)
    return d * lax.rsqrt(var + LN_EPS) * g + b


def _softplus(z):
    return jnp.maximum(z, 0.0) + jnp.log(1.0 + jnp.exp(-jnp.abs(z)))


LOG2_E = 1.4426950408889634


def _softplus_base2(z2):
    return jnp.maximum(z2, 0.0) + jnp.log2(1.0 + jnp.exp2(-jnp.abs(z2)))


def _top_blocks(gate, idx_f, n_idx, n_valid, axis):
    selected = jnp.zeros(gate.shape, F32)
    g = gate
    for r in range(min(MOBA_TOPK, n_idx)):
        mx = jnp.max(g, axis=axis, keepdims=True)
        idx = jnp.min(jnp.where(g == mx, idx_f, float(n_idx)), axis=axis, keepdims=True)
        hit = idx_f == idx
        selected = jnp.where(jnp.logical_and(hit, r < n_valid), 1.0, selected)
        g = jnp.where(hit, -jnp.inf, g)
    return selected


def _mod_kernel(c_ref, w_ref, b_ref, o_ref):
    c = c_ref[...]
    s = c * jax.nn.sigmoid(c)
    o_ref[...] = jnp.dot(s, w_ref[...], preferred_element_type=F32, precision=HIGHEST) + b_ref[...]


def _modulation(c_all, w_mod, b_mod):
    depth, d, n = w_mod.shape
    rows = c_all.shape[0]
    tn = n // 8
    return pl.pallas_call(
        _mod_kernel,
        grid=(depth, n // tn),
        in_specs=[pl.BlockSpec((rows, d), lambda l, j: (0, 0)),
                  pl.BlockSpec((None, d, tn), lambda l, j: (l, 0, j)),
                  pl.BlockSpec((None, 1, tn), lambda l, j: (l, 0, j))],
        out_specs=pl.BlockSpec((None, rows, tn), lambda l, j: (l, 0, j)),
        out_shape=jax.ShapeDtypeStruct((depth, rows, n), F32),
        compiler_params=_params(2),
        name="ada_modulation",
    )(c_all, w_mod, b_mod.reshape(depth, 1, n))


def _ffn_kernel(x_ref, sh_ref, sc_ref, gt_ref, wgu_ref, wd_ref, lg_ref, lb_ref, o_ref, *,
                d_ff, chunk, alpha):
    x = x_ref[...]
    h = (x * (1.0 + sc_ref[...]) + sh_ref[...]).astype(BF16)
    acc = jnp.zeros(x.shape, F32)
    for c0 in range(0, d_ff, chunk):
        g = jnp.dot(h, wgu_ref[:, c0:c0 + chunk], preferred_element_type=F32)
        u = jnp.dot(h, wgu_ref[:, d_ff + c0:d_ff + c0 + chunk], preferred_element_type=F32)
        act = (g * jax.nn.sigmoid(g) * u).astype(BF16)
        acc = acc + jnp.dot(act, wd_ref[c0:c0 + chunk, :], preferred_element_type=F32)
    y = alpha * x + FFN_RESIDUAL * gt_ref[...] * acc
    o_ref[...] = _layer_norm(y, lg_ref[...], lb_ref[...])


def _mod_spec(mod_rows, tm, tiles_per_seq, d):
    if mod_rows == 1:
        return pl.BlockSpec((None, 1, d), lambda i: (i // tiles_per_seq, 0, 0))
    return pl.BlockSpec((None, tm, d), lambda i: (i, 0, 0))


def _ffn(x, shift, scale, gate, w_gu, w_down, ln_g, ln_b, *, tm, tiles_per_seq, alpha):
    n_tok, d = x.shape
    d_ff = w_down.shape[0]
    chunk = d_ff // 2
    mspec = _mod_spec(shift.shape[1], tm, tiles_per_seq, d)
    row = pl.BlockSpec((tm, d), lambda i: (i, 0))
    return pl.pallas_call(
        functools.partial(_ffn_kernel, d_ff=d_ff, chunk=chunk, alpha=alpha),
        grid=(n_tok // tm,),
        in_specs=[row, mspec, mspec, mspec,
                  _resident((d, 2 * d_ff), lambda i: (0, 0)),
                  _resident((d_ff, d), lambda i: (0, 0)),
                  _resident((1, d), lambda i: (0, 0)),
                  _resident((1, d), lambda i: (0, 0))],
        out_specs=row,
        out_shape=jax.ShapeDtypeStruct((n_tok, d), F32),
        compiler_params=_params(1),
        name="ffn_sublayer",
    )(x, shift, scale, gate, w_gu, w_down, ln_g, ln_b)


def _qkv_sample_kernel(x_ref, sh_ref, sc_ref, w_ref, q_ref, k_ref, v_ref, *, mix):
    h = (x_ref[...] * (1.0 + sc_ref[...]) + sh_ref[...]).astype(BF16)
    qkv = jnp.dot(h, w_ref[...], preferred_element_type=F32)
    q_ref[...] = qkv[:, :mix]
    k_ref[...] = qkv[:, mix:2 * mix]
    v_ref[...] = qkv[:, 2 * mix:]


def _qkv_sample(x, shift, scale, w_in):
    n_tok, d = x.shape
    mix = w_in.shape[1] // 3
    full = lambda shape: pl.BlockSpec(shape, lambda i: (0,) * len(shape))
    mspec = _mod_spec(n_tok, n_tok, 1, d)
    return pl.pallas_call(
        functools.partial(_qkv_sample_kernel, mix=mix),
        grid=(1,),
        in_specs=[full((n_tok, d)), mspec, mspec, full((d, 3 * mix))],
        out_specs=[full((n_tok, mix))] * 3,
        out_shape=[jax.ShapeDtypeStruct((n_tok, mix), F32)] * 3,
        compiler_params=_params(1),
        name="qkv_projection_sample",
    )(x, shift, scale, w_in)


def _qkv_prompt_kernel(x_ref, sh_ref, sc_ref, wq_ref, wkvt_ref, *refs, mix, sb_width, inv_sqrt, n_alias):
    q32_ref, qb_ref, kt_ref, vt_ref, ktb_ref, vtb_ref, km_ref = refs[n_alias:]
    h = (x_ref[...] * (1.0 + sc_ref[...]) + sh_ref[...]).astype(BF16)
    q = jnp.dot(h, wq_ref[...], preferred_element_type=F32)
    q32_ref[...] = q[:, sb_width:]
    qb_ref[...] = (q * inv_sqrt).astype(BF16)
    kvt = _dot_t(wkvt_ref[...], h)
    kt, vt = kvt[:mix], kvt[mix:]
    kt_ref[...] = kt
    vt_ref[...] = vt
    for j in range(kt.shape[1] // MOBA_BLOCK):
        kj = kt[:, j * MOBA_BLOCK:(j + 1) * MOBA_BLOCK]
        ktb_ref[j] = kj.astype(BF16)
        vtb_ref[j] = vt[:, j * MOBA_BLOCK:(j + 1) * MOBA_BLOCK].astype(BF16)
        km_ref[:, j:j + 1] = jnp.sum(kj, axis=1, keepdims=True) * (1.0 / MOBA_BLOCK)


def _qkv_prompt(x, shift, scale, w_q, w_kv_t, kt_all, vt_all, *, layer, depth, batch, seq, tm, head_dim,
                sb_width):
    n_tok, d = x.shape
    mix = w_q.shape[1]
    tiles_per_seq = seq // tm
    blk_per_tile = tm // MOBA_BLOCK
    n_kb = seq // MOBA_BLOCK
    n_alias = 0 if kt_all is None else 2
    mspec = _mod_spec(1, tm, tiles_per_seq, d)
    row = pl.BlockSpec((tm, d), lambda i: (i, 0))
    stacked = pl.BlockSpec((None, None, mix, tm), lambda i: (layer, i // tiles_per_seq, 0, i % tiles_per_seq))
    blocked = pl.BlockSpec((None, blk_per_tile, mix, MOBA_BLOCK),
                           lambda i: (i // tiles_per_seq, i % tiles_per_seq, 0, 0))
    stacked_shape = jax.ShapeDtypeStruct((depth, batch, mix, seq), F32)
    blocked_shape = jax.ShapeDtypeStruct((batch, n_kb, mix, MOBA_BLOCK), BF16)
    in_specs = [row, mspec, mspec, _resident((d, mix), lambda i: (0, 0)),
                _resident((2 * mix, d), lambda i: (0, 0))]
    args = [x, shift, scale, w_q, w_kv_t]
    if n_alias:
        in_specs += [pl.BlockSpec(memory_space=pl.ANY)] * 2
        args += [kt_all, vt_all]
    return pl.pallas_call(
        functools.partial(_qkv_prompt_kernel, mix=mix, sb_width=sb_width, inv_sqrt=head_dim ** -0.5 * LOG2_E,
                          n_alias=n_alias),
        grid=(n_tok // tm,),
        in_specs=in_specs,
        out_specs=[pl.BlockSpec((tm, mix - sb_width), lambda i: (i, 0)),
                   pl.BlockSpec((tm, mix), lambda i: (i, 0)),
                   stacked, stacked, blocked, blocked,
                   pl.BlockSpec((None, mix, blk_per_tile), lambda i: (i, 0, 0))],
        out_shape=[jax.ShapeDtypeStruct((n_tok, mix - sb_width), F32),
                   jax.ShapeDtypeStruct((n_tok, mix), BF16),
                   stacked_shape, stacked_shape, blocked_shape, blocked_shape,
                   jax.ShapeDtypeStruct((n_tok // tm, mix, blk_per_tile), F32)],
        input_output_aliases={5: 2, 6: 3} if n_alias else {},
        compiler_params=_params(1),
        name="qkv_projection_prompt",
    )(*args)


ATTN_COLS_PER_STEP = 4


def _sb_prompt_kernel(q_ref, k_ref, v_ref, tri_ref, o_ref, *, tq, head_dim, cols):
    qi = pl.program_id(2)
    tri = tri_ref[...]
    lane_head = lax.broadcasted_iota(jnp.int32, (1, LANES), 1) // head_dim
    row = lax.broadcasted_iota(jnp.int32, (tq, tq), 0)
    col = lax.broadcasted_iota(jnp.int32, (tq, tq), 1)
    past = col < row

    n_h = LANES // head_dim
    qhs = []
    for cc in range(cols):
        q = q_ref[:, cc * LANES:(cc + 1) * LANES]
        qhs.append([jnp.where(lane_head == hh, q, jnp.zeros_like(q)) for hh in range(n_h)])

    def key_block(kj, carry, diagonal):
        cs, accs = carry
        new_cs, new_accs = [], []
        for cc in range(cols):
            kb = k_ref[kj, cc * LANES:(cc + 1) * LANES, :]
            vb = v_ref[kj, cc * LANES:(cc + 1) * LANES, :]
            pv = None
            for hh in range(n_h):
                c = cs[cc * n_h + hh]
                z = jnp.dot(qhs[cc][hh], kb, preferred_element_type=F32)
                sp = _softplus_base2(z)
                log_keep = -sp
                log_beta = z - sp
                if diagonal:
                    log_keep = jnp.where(past, log_keep, 0.0)
                suffix = jnp.dot(log_keep.astype(BF16), tri, preferred_element_type=F32) + c
                a = jnp.exp2(log_beta + suffix)
                if diagonal:
                    a = jnp.where(past, a, 0.0)
                pv_h = _dot_t(a.astype(BF16), vb)
                pv = pv_h if pv is None else jnp.where(lane_head == hh, pv_h, pv)
                new_cs.append(c + jnp.sum(log_keep, axis=1, keepdims=True))
            new_accs.append(accs[cc] + pv)
        return tuple(new_cs), tuple(new_accs)

    carry = (tuple(jnp.zeros((tq, 1), F32) for _ in range(cols * n_h)),
             tuple(jnp.zeros((tq, LANES), F32) for _ in range(cols)))
    carry = key_block(qi, carry, True)
    carry = lax.fori_loop(0, qi, lambda i, cr: key_block(qi - 1 - i, cr, False), carry)
    for cc in range(cols):
        o_ref[:, cc * LANES:(cc + 1) * LANES] = carry[1][cc]


def _attn_cols(n_cols):
    return ATTN_COLS_PER_STEP if n_cols % ATTN_COLS_PER_STEP == 0 else 1


def _kv_block_spec(n_kb, cols, col0):
    return pl.BlockSpec((None, n_kb, cols * LANES, MOBA_BLOCK), lambda b, c, i: (b, 0, col0 + c, 0))


def _sb_prompt(qb, ktb, vtb, tri, *, n_cols, head_dim):
    batch, n_kb = ktb.shape[:2]
    tq = MOBA_BLOCK
    cols = _attn_cols(n_cols)
    tile = pl.BlockSpec((tq, cols * LANES), lambda b, c, i: (b * n_kb + i, c))
    return pl.pallas_call(
        functools.partial(_sb_prompt_kernel, tq=tq, head_dim=head_dim, cols=cols),
        grid=(batch, n_cols // cols, n_kb),
        in_specs=[tile, _kv_block_spec(n_kb, cols, 0), _kv_block_spec(n_kb, cols, 0),
                  _resident((tq, tq), lambda b, c, i: (0, 0))],
        out_specs=tile,
        out_shape=jax.ShapeDtypeStruct((batch * n_kb * tq, n_cols * LANES), F32),
        compiler_params=_params(3),
        name="sb_attention_prompt",
    )(qb, ktb, vtb, tri)


def _moba_prompt_kernel(slopes_ref, q_ref, q32_ref, k_ref, v_ref, km_ref, o_ref, *,
                        tq, head_dim, n_blocks, cols):
    cb = pl.program_id(1)
    qi = pl.program_id(2)
    n_h = LANES // head_dim
    lane_head = lax.broadcasted_iota(jnp.int32, (1, LANES), 1) // head_dim
    feat_head = lax.broadcasted_iota(jnp.int32, (LANES, 1), 0) // head_dim
    blk = lax.broadcasted_iota(jnp.int32, (tq, n_blocks), 1)
    row = lax.broadcasted_iota(jnp.int32, (tq, tq), 0)
    col = lax.broadcasted_iota(jnp.int32, (tq, tq), 1)
    rel = (row - col).astype(F32)

    qhs, slopes, selected = [], [], []
    for cc in range(cols):
        q = q_ref[:, cc * LANES:(cc + 1) * LANES]
        q32 = q32_ref[:, cc * LANES:(cc + 1) * LANES]
        km = km_ref[cc * LANES:(cc + 1) * LANES, :]
        for hh in range(n_h):
            qhs.append(jnp.where(lane_head == hh, q, jnp.zeros_like(q)))
            slopes.append(slopes_ref[(cb * cols + cc) * n_h + hh])
            gate = jnp.dot(jnp.where(lane_head == hh, q32, 0.0), jnp.where(feat_head == hh, km, 0.0),
                           preferred_element_type=F32, precision=HIGHEST)
            gate = jnp.where(blk < qi, gate, -jnp.inf)
            selected.append(_top_blocks(gate, blk.astype(F32), n_blocks, qi, axis=1))

    def key_block(kj, carry, own):
        dist = rel + ((qi - kj) * tq).astype(F32)
        new = []
        for cc in range(cols):
            kb = k_ref[kj, cc * LANES:(cc + 1) * LANES, :]
            vb = v_ref[kj, cc * LANES:(cc + 1) * LANES, :]
            for hh in range(n_h):
                h = cc * n_h + hh
                m, l, acc = carry[h]
                s = jnp.dot(qhs[h], kb, preferred_element_type=F32) - slopes[h] * dist
                if own:
                    s = jnp.where(rel >= 0.0, s, NEG_BIG)
                else:
                    picked = jnp.max(jnp.where(blk == kj, selected[h], 0.0), axis=1, keepdims=True)
                    s = jnp.where(picked > 0.0, s, NEG_BIG)
                m_new = jnp.maximum(m, jnp.max(s, axis=1, keepdims=True))
                scale = jnp.exp2(m - m_new)
                p = jnp.exp2(s - m_new)
                l = scale * l + jnp.sum(p, axis=1, keepdims=True)
                acc = scale * acc + _dot_t(p.astype(BF16), vb)
                new.append((m_new, l, acc))
        return tuple(new)

    carry = tuple((jnp.full((tq, 1), NEG_BIG, F32), jnp.zeros((tq, 1), F32), jnp.zeros((tq, LANES), F32))
                  for _ in range(cols * n_h))
    carry = key_block(qi, carry, True)
    carry = lax.fori_loop(0, qi, lambda j, cr: key_block(j, cr, False), carry)
    for cc in range(cols):
        o = carry[cc * n_h][2] / carry[cc * n_h][1]
        for hh in range(1, n_h):
            o = jnp.where(lane_head == hh, carry[cc * n_h + hh][2] / carry[cc * n_h + hh][1], o)
        o_ref[:, cc * LANES:(cc + 1) * LANES] = o


def _moba_prompt(slopes, qb, q32, ktb, vtb, kmean_t, *, col0, n_cols, head_dim):
    batch, n_kb = ktb.shape[:2]
    tq = MOBA_BLOCK
    cols = _attn_cols(n_cols)
    assert col0 % cols == 0
    c0 = col0 // cols
    return pl.pallas_call(
        functools.partial(_moba_prompt_kernel, tq=tq, head_dim=head_dim, n_blocks=n_kb, cols=cols),
        grid=(batch, n_cols // cols, n_kb),
        in_specs=[pl.BlockSpec(memory_space=pltpu.SMEM),
                  pl.BlockSpec((tq, cols * LANES), lambda b, c, i: (b * n_kb + i, c0 + c)),
                  pl.BlockSpec((tq, cols * LANES), lambda b, c, i: (b * n_kb + i, c)),
                  _kv_block_spec(n_kb, cols, c0), _kv_block_spec(n_kb, cols, c0),
                  pl.BlockSpec((None, cols * LANES, n_kb), lambda b, c, i: (b, c0 + c, 0))],
        out_specs=pl.BlockSpec((tq, cols * LANES), lambda b, c, i: (b * n_kb + i, c)),
        out_shape=jax.ShapeDtypeStruct((batch * n_kb * tq, n_cols * LANES), F32),
        compiler_params=_params(3),
        name="moba_attention_prompt",
    )(slopes, qb, q32, ktb, vtb, kmean_t)


def _outproj_kernel(x_ref, osb_ref, omb_ref, gt_ref, gn_ref, w_ref, lg_ref, lb_ref, o_ref, *, alpha):
    x = x_ref[...]
    gn = gn_ref[...]
    w_sb = osb_ref.shape[1]

    def group_norm(o, g):
        return (o * lax.rsqrt(jnp.mean(o * o, axis=-1, keepdims=True) + RMS_EPS) * g).astype(BF16)

    n_sb = group_norm(osb_ref[...], gn[:, :w_sb])
    n_mb = group_norm(omb_ref[...], gn[:, w_sb:])
    mix = (jnp.dot(n_sb, w_ref[:w_sb, :], preferred_element_type=F32)
           + jnp.dot(n_mb, w_ref[w_sb:, :], preferred_element_type=F32))
    y = alpha * x + gt_ref[...] * mix
    o_ref[...] = _layer_norm(y, lg_ref[...], lb_ref[...])


def _outproj(x, o_sb, o_mb, gate, w_gn, w_out, ln_g, ln_b, *, tm, tiles_per_seq, alpha):
    n_tok, d = x.shape
    mspec = _mod_spec(gate.shape[1], tm, tiles_per_seq, d)
    row = pl.BlockSpec((tm, d), lambda i: (i, 0))
    return pl.pallas_call(
        functools.partial(_outproj_kernel, alpha=alpha),
        grid=(n_tok // tm,),
        in_specs=[row,
                  pl.BlockSpec((tm, o_sb.shape[1]), lambda i: (i, 0)),
                  pl.BlockSpec((tm, o_mb.shape[1]), lambda i: (i, 0)),
                  mspec,
                  _resident((1, w_gn.shape[1]), lambda i: (0, 0)),
                  _resident(w_out.shape, lambda i: (0, 0)),
                  _resident((1, d), lambda i: (0, 0)),
                  _resident((1, d), lambda i: (0, 0))],
        out_specs=row,
        out_shape=jax.ShapeDtypeStruct((n_tok, d), F32),
        compiler_params=_params(1),
        name="mix_outproj",
    )(x, o_sb, o_mb, gate, w_gn, w_out, ln_g, ln_b)


SAMPLE_BLOCKS_PER_STEP = 2


def _sample_attn_kernel(pt_ref, qc_ref, q_ref, kn_ref, vnt_ref, slope_ref, tri_ref, *refs,
                        n_blk, blk_per_step, pages_per_blk, n_heads, n_sb, head_dim, past_len):
    del pt_ref
    n_pg = blk_per_step * pages_per_blk
    k_refs = refs[:n_pg]
    v_refs = refs[n_pg:2 * n_pg]
    o_ref = refs[2 * n_pg]
    c_sc, osb_sc, gate_sc, m_sc, l_sc, omb_sc = refs[2 * n_pg + 1:]

    j = pl.program_id(1)
    n_mb = n_heads - n_sb
    inv_sqrt = head_dim ** -0.5

    @pl.when(j == 0)
    def _():
        c_sc[...] = jnp.zeros_like(c_sc)
        osb_sc[...] = jnp.zeros_like(osb_sc)
        gate_sc[...] = jnp.zeros_like(gate_sc)
        m_sc[...] = jnp.zeros_like(m_sc)
        l_sc[...] = jnp.zeros_like(l_sc)

    mb_r = lax.broadcasted_iota(jnp.int32, (n_mb, n_heads), 0)
    mb_l = lax.broadcasted_iota(jnp.int32, (n_mb, n_heads), 1)
    mb_slot = jnp.where(mb_l == mb_r + n_sb, 1.0, 0.0)

    def to_lanes(colv):
        return jnp.sum(mb_slot * colv, axis=0, keepdims=True)

    qc = qc_ref[...]
    slope = slope_ref[...]
    tok = lax.broadcasted_iota(jnp.int32, (1, MOBA_BLOCK), 1)
    blk_row = lax.broadcasted_iota(jnp.int32, (n_blk, n_heads), 0)
    blks = [n_blk - 1 - (j * blk_per_step + sb) for sb in range(blk_per_step)]

    zraws = [jnp.concatenate([jnp.sum(kp[...] * qc, axis=1)
                              for kp in k_refs[sb * pages_per_blk:(sb + 1) * pages_per_blk]], axis=1)
             for sb in range(blk_per_step)]

    z_sb = [zr[:n_sb] * inv_sqrt for zr in zraws]
    sps = [_softplus(z) for z in z_sb]
    log_keep = jnp.concatenate([-sp for sp in sps], axis=0)
    hi = log_keep.astype(BF16)
    lo = (log_keep - hi.astype(F32)).astype(BF16)
    sums = jnp.dot(jnp.concatenate([hi, lo], axis=0), tri_ref[...], preferred_element_type=F32)
    suffix = sums[:blk_per_step * n_sb] + sums[blk_per_step * n_sb:]
    c = c_sc[...]
    weights = []
    for sb in range(blk_per_step):
        a_sb = jnp.exp(z_sb[sb] - sps[sb] + suffix[sb * n_sb:(sb + 1) * n_sb] + c)
        c = c - jnp.sum(sps[sb], axis=1, keepdims=True)

        dist = (past_len - blks[sb] * MOBA_BLOCK - tok).astype(F32)
        s = zraws[sb][n_sb:] * inv_sqrt - slope * dist
        m_blk = jnp.max(s, axis=1, keepdims=True)
        p_mb = jnp.exp(s - m_blk)
        l_blk = jnp.sum(p_mb, axis=1, keepdims=True)
        g_blk = jnp.sum(zraws[sb][n_sb:], axis=1, keepdims=True) * (1.0 / MOBA_BLOCK)
        here = blk_row == blks[sb]
        m_sc[...] = jnp.where(here, to_lanes(m_blk), m_sc[...])
        l_sc[...] = jnp.where(here, to_lanes(l_blk), l_sc[...])
        gate_sc[...] = jnp.where(here, to_lanes(g_blk), gate_sc[...])
        weights.append(jnp.concatenate([a_sb, p_mb], axis=0).astype(BF16))
    c_sc[...] = c

    own_r = lax.broadcasted_iota(jnp.int32, (n_heads, head_dim, n_heads), 0)
    own_c = lax.broadcasted_iota(jnp.int32, (n_heads, head_dim, n_heads), 2)
    acc_step = jnp.zeros((head_dim, n_heads), F32)
    for sb in range(blk_per_step):
        vt = jnp.concatenate([vp[...].reshape(n_heads * head_dim, vp.shape[2])
                              for vp in v_refs[sb * pages_per_blk:(sb + 1) * pages_per_blk]], axis=1)
        r_all = _dot_t(vt.astype(BF16), weights[sb]).reshape(n_heads, head_dim, n_heads)
        acc = jnp.sum(jnp.where(own_r == own_c, r_all, 0.0), axis=0)
        acc_step = acc_step + acc
        omb_sc[blks[sb]] = acc
    osb_sc[...] = osb_sc[...] + acc_step

    @pl.when(j == n_blk // blk_per_step - 1)
    def _():
        blk_f = lax.broadcasted_iota(jnp.int32, (n_blk, n_heads), 0).astype(F32)
        selected = _top_blocks(gate_sc[...], blk_f, n_blk + 1, n_blk, axis=0)
        s_self = to_lanes(jnp.sum(q_ref[...] * kn_ref[...], axis=1, keepdims=True)[n_sb:]) * inv_sqrt
        m_fin = jnp.maximum(jnp.max(jnp.where(selected > 0.0, m_sc[...], NEG_BIG), axis=0, keepdims=True),
                            s_self)
        coef = jnp.where(selected > 0.0, jnp.exp(m_sc[...] - m_fin), 0.0)
        w_self = jnp.exp(s_self - m_fin)
        denom = jnp.sum(coef * l_sc[...], axis=0, keepdims=True) + w_self
        o_mb = w_self * vnt_ref[...]
        for b in range(n_blk):
            o_mb = o_mb + coef[b:b + 1, :] * omb_sc[b]
        head = lax.broadcasted_iota(jnp.int32, (1, n_heads), 1)
        o_ref[...] = jnp.where(head < n_sb, osb_sc[...], o_mb / denom)


def _sample_attention(layer, page_table, q_col, q, k_new, v_new_t, slope_col, tri, cache_kt, cache_vt, *, n_sb):
    db, n_heads, head_dim = q.shape
    page = cache_kt.shape[4]
    n_pages = page_table.shape[1]
    pages_per_blk = MOBA_BLOCK // page
    n_blk = n_pages // pages_per_blk
    past_len = n_pages * page
    n_mb = n_heads - n_sb
    blk_per_step = SAMPLE_BLOCKS_PER_STEP if n_blk % SAMPLE_BLOCKS_PER_STEP == 0 else 1
    n_pg = blk_per_step * pages_per_blk

    def page_spec(p):
        def index(b, j, pt):
            blk = n_blk - 1 - (j * blk_per_step + p // pages_per_blk)
            return (layer, pt[b * n_pages + blk * pages_per_blk + p % pages_per_blk], 0, 0, 0)
        return pl.BlockSpec((None, None, n_heads, head_dim, page), index)

    per_seq = lambda *shape: pl.BlockSpec((None,) + shape, lambda b, j, pt: (b,) + (0,) * len(shape))
    grid_spec = pltpu.PrefetchScalarGridSpec(
        num_scalar_prefetch=1,
        grid=(db, n_blk // blk_per_step),
        in_specs=[per_seq(n_heads, head_dim, 1), per_seq(n_heads, head_dim), per_seq(n_heads, head_dim),
                  per_seq(head_dim, n_heads),
                  pl.BlockSpec((n_mb, 1), lambda b, j, pt: (0, 0)),
                  pl.BlockSpec((MOBA_BLOCK, MOBA_BLOCK), lambda b, j, pt: (0, 0))]
                 + [page_spec(p) for p in range(n_pg)] * 2,
        out_specs=per_seq(head_dim, n_heads),
        scratch_shapes=[pltpu.VMEM((n_sb, 1), F32),
                        pltpu.VMEM((head_dim, n_heads), F32),
                        pltpu.VMEM((n_blk, n_heads), F32),
                        pltpu.VMEM((n_blk, n_heads), F32),
                        pltpu.VMEM((n_blk, n_heads), F32),
                        pltpu.VMEM((n_blk, head_dim, n_heads), F32)])
    return pl.pallas_call(
        functools.partial(_sample_attn_kernel, n_blk=n_blk, blk_per_step=blk_per_step,
                          pages_per_blk=pages_per_blk, n_heads=n_heads, n_sb=n_sb, head_dim=head_dim,
                          past_len=past_len),
        grid_spec=grid_spec,
        out_shape=jax.ShapeDtypeStruct((db, head_dim, n_heads), F32),
        compiler_params=_params(2),
        name="sample_paged_attention",
    )(page_table.reshape(-1), q_col, q, k_new, v_new_t, slope_col, tri,
      *([cache_kt] * n_pg), *([cache_vt] * n_pg))


def _alibi_slopes(n):
    return np.asarray(2.0 ** (-8.0 * np.arange(1, n + 1) / n), dtype=np.float32)


def kernel(x_prompt, x_sample, cache_k, cache_v, page_table, c_prompt, c_sample, w_mod, b_mod,
           ln_g, ln_b, w_ffn_gu, w_ffn_down, w_in, w_group_norm, w_out):
    batch, seq, d = x_prompt.shape
    db, dec_seq, _ = x_sample.shape
    depth, _, page, n_heads, head_dim = cache_k.shape
    mix = n_heads * head_dim
    n_sb = n_heads // 2
    n_mb = n_heads - n_sb
    sb_width = n_sb * head_dim
    alpha = (2.0 * depth) ** 0.25
    assert dec_seq == 1 and mix == d
    assert seq % MOBA_BLOCK == 0 and MOBA_BLOCK % page == 0
    assert (page_table.shape[1] * page) % MOBA_BLOCK == 0
    assert sb_width % LANES == 0 and LANES % head_dim == 0

    tm = 512 if seq % 512 == 0 else MOBA_BLOCK
    tiles_per_seq = seq // tm
    n_tok = batch * seq
    n_kb = seq // MOBA_BLOCK
    sb_cols = sb_width // LANES
    mb_cols = (mix - sb_width) // LANES

    slopes = jnp.asarray(_alibi_slopes(n_mb))
    tri_np = np.tril(np.ones((MOBA_BLOCK, MOBA_BLOCK), np.float32), -1)
    tri_bf16 = jnp.asarray(tri_np, dtype=BF16)

    w_gu_b = w_ffn_gu.astype(BF16)
    w_dn_b = w_ffn_down.astype(BF16)
    w_in_b = w_in.astype(BF16)
    w_q_b = w_in_b[:, :, :mix]
    w_kv_t_b = jnp.swapaxes(w_in_b[:, :, mix:], 1, 2)
    w_out_b = w_out.astype(BF16)

    mod = _modulation(jnp.concatenate([c_prompt, c_sample], axis=0), w_mod, b_mod)
    mod = mod.reshape(depth, batch + db, N_SUBLAYERS, 3, d)

    cache_kt = jnp.transpose(cache_k, (0, 1, 3, 4, 2))
    cache_vt = jnp.transpose(cache_v, (0, 1, 3, 4, 2))

    xp = x_prompt.reshape(n_tok, d)
    xs = x_sample.reshape(db, d)
    kt_all = vt_all = None
    sk, sv = [], []
    for layer in range(depth):
        def mods(sub, prompt):
            if prompt:
                m = mod[layer, :batch, sub]
                return tuple(m[:, i][:, None, :] for i in range(3))
            m = mod[layer, batch:, sub]
            return tuple(m[:, i][None, :, :] for i in range(3))

        def lnp(sub):
            return ln_g[layer, sub][None, :], ln_b[layer, sub][None, :]

        def ffn(x, sub, which, prompt):
            sh, sc, gt = mods(sub, prompt)
            g, b = lnp(sub)
            return _ffn(x, sh, sc, gt, w_gu_b[layer, which], w_dn_b[layer, which], g, b,
                        tm=tm if prompt else db, tiles_per_seq=tiles_per_seq, alpha=alpha)

        xp = ffn(xp, 0, 0, True)
        sh, sc, gt = mods(1, True)
        q32, qb, kt_all, vt_all, ktb, vtb, km = _qkv_prompt(
            xp, sh, sc, w_q_b[layer], w_kv_t_b[layer], kt_all, vt_all, layer=layer, depth=depth,
            batch=batch, seq=seq, tm=tm, head_dim=head_dim, sb_width=sb_width)
        kmean_t = jnp.transpose(km.reshape(batch, tiles_per_seq, mix, tm // MOBA_BLOCK),
                                (0, 2, 1, 3)).reshape(batch, mix, n_kb)
        o_sb = _sb_prompt(qb, ktb, vtb, tri_bf16, n_cols=sb_cols, head_dim=head_dim)
        o_mb = _moba_prompt(slopes * LOG2_E, qb, q32, ktb, vtb, kmean_t, col0=sb_cols, n_cols=mb_cols,
                            head_dim=head_dim)
        g, b = lnp(1)
        xp = _outproj(xp, o_sb, o_mb, gt, w_group_norm[layer][None, :], w_out_b[layer], g, b,
                      tm=tm, tiles_per_seq=tiles_per_seq, alpha=alpha)
        xp = ffn(xp, 2, 1, True)

        xs = ffn(xs, 0, 0, False)
        sh, sc, gt = mods(1, False)
        q32, k32, v32 = _qkv_sample(xs, sh, sc, w_in_b[layer])
        q3 = q32.reshape(db, n_heads, head_dim)
        k3 = k32.reshape(db, n_heads, head_dim)
        v3 = v32.reshape(db, n_heads, head_dim)
        o_t = _sample_attention(layer, page_table, q3[..., None], q3, k3, jnp.swapaxes(v3, 1, 2),
                                slopes.reshape(n_mb, 1), tri_bf16, cache_kt, cache_vt, n_sb=n_sb)
        o = jnp.swapaxes(o_t, 1, 2).reshape(db, mix)
        g, b = lnp(1)
        xs = _outproj(xs, o[:, :sb_width], o[:, sb_width:], gt, w_group_norm[layer][None, :], w_out_b[layer],
                      g, b, tm=db, tiles_per_seq=1, alpha=alpha)
        xs = ffn(xs, 2, 1, False)
        sk.append(k3.reshape(db, 1, n_heads, head_dim))
        sv.append(v3.reshape(db, 1, n_heads, head_dim))

    def token_major(t_all):
        return jnp.transpose(t_all.reshape(depth, batch, n_heads, head_dim, seq), (0, 1, 4, 2, 3))

    return (xp.reshape(batch, seq, d), xs.reshape(db, 1, d),
            token_major(kt_all), token_major(vt_all), jnp.stack(sk), jnp.stack(sv))
```

```python
import functools

import jax
import jax.numpy as jnp
import numpy as np
from jax import lax
from jax.experimental import pallas as pl
from jax.experimental.pallas import tpu as pltpu

N_SUBLAYERS = 3
FFN_RESIDUAL = 0.5
LN_EPS = 1e-5
RMS_EPS = 1e-6
MOBA_BLOCK = 256
MOBA_TOPK = 3
LANES = 128
NEG_BIG = -1e30
VMEM_LIMIT = 56 * 1024 * 1024

F32 = jnp.float32
BF16 = jnp.bfloat16
HIGHEST = lax.Precision.HIGHEST


def _params(n_grid_dims):
    return pltpu.CompilerParams(
        dimension_semantics=("arbitrary",) * n_grid_dims,
        vmem_limit_bytes=VMEM_LIMIT)


def _resident(block_shape, index_map):
    return pl.BlockSpec(block_shape, index_map, pipeline_mode=pl.Buffered(1))


def _dot_t(a, b, precision=None):
    return lax.dot_general(a, b, (((1,), (1,)), ((), ())),
                           preferred_element_type=F32, precision=precision)


def _layer_norm(y, g, b):
    mu = jnp.mean(y, axis=-1, keepdims=True)
    d = y - mu
    var = jnp.mean(d * d, axis=-1, keepdims=True)
    return d * lax.rsqrt(var + LN_EPS) * g + b


def _softplus(z):
    return jnp.maximum(z, 0.0) + jnp.log(1.0 + jnp.exp(-jnp.abs(z)))


LOG2_E = 1.4426950408889634


def _softplus_base2(z2):
    return jnp.maximum(z2, 0.0) + jnp.log2(1.0 + jnp.exp2(-jnp.abs(z2)))


def _top_blocks(gate, idx_f, n_idx, n_valid, axis):
    selected = jnp.zeros(gate.shape, F32)
    g = gate
    for r in range(min(MOBA_TOPK, n_idx)):
        mx = jnp.max(g, axis=axis, keepdims=True)
        idx = jnp.min(jnp.where(g == mx, idx_f, float(n_idx)), axis=axis, keepdims=True)
        hit = idx_f == idx
        selected = jnp.where(jnp.logical_and(hit, r < n_valid), 1.0, selected)
        g = jnp.where(hit, -jnp.inf, g)
    return selected


def _mod_kernel(c_ref, w_ref, b_ref, o_ref):
    c = c_ref[...]
    s = c * jax.nn.sigmoid(c)
    o_ref[...] = jnp.dot(s, w_ref[...], preferred_element_type=F32, precision=HIGHEST) + b_ref[...]


def _modulation(c_all, w_mod, b_mod):
    depth, d, n = w_mod.shape
    rows = c_all.shape[0]
    tn = n // 8
    return pl.pallas_call(
        _mod_kernel,
        grid=(depth, n // tn),
        in_specs=[pl.BlockSpec((rows, d), lambda l, j: (0, 0)),
                  pl.BlockSpec((None, d, tn), lambda l, j: (l, 0, j)),
                  pl.BlockSpec((None, 1, tn), lambda l, j: (l, 0, j))],
        out_specs=pl.BlockSpec((None, rows, tn), lambda l, j: (l, 0, j)),
        out_shape=jax.ShapeDtypeStruct((depth, rows, n), F32),
        compiler_params=_params(2),
        name="ada_modulation",
    )(c_all, w_mod, b_mod.reshape(depth, 1, n))


def _ffn_kernel(x_ref, sh_ref, sc_ref, gt_ref, wgu_ref, wd_ref, lg_ref, lb_ref, o_ref, *,
                d_ff, chunk, alpha):
    x = x_ref[...]
    h = (x * (1.0 + sc_ref[...]) + sh_ref[...]).astype(BF16)
    acc = jnp.zeros(x.shape, F32)
    for c0 in range(0, d_ff, chunk):
        g = jnp.dot(h, wgu_ref[:, c0:c0 + chunk], preferred_element_type=F32)
        u = jnp.dot(h, wgu_ref[:, d_ff + c0:d_ff + c0 + chunk], preferred_element_type=F32)
        act = (g * jax.nn.sigmoid(g) * u).astype(BF16)
        acc = acc + jnp.dot(act, wd_ref[c0:c0 + chunk, :], preferred_element_type=F32)
    y = alpha * x + FFN_RESIDUAL * gt_ref[...] * acc
    o_ref[...] = _layer_norm(y, lg_ref[...], lb_ref[...])


def _mod_spec(mod_rows, tm, tiles_per_seq, d):
    if mod_rows == 1:
        return pl.BlockSpec((None, 1, d), lambda i: (i // tiles_per_seq, 0, 0))
    return pl.BlockSpec((None, tm, d), lambda i: (i, 0, 0))


def _ffn(x, shift, scale, gate, w_gu, w_down, ln_g, ln_b, *, tm, tiles_per_seq, alpha):
    n_tok, d = x.shape
    d_ff = w_down.shape[0]
    chunk = d_ff // 2
    mspec = _mod_spec(shift.shape[1], tm, tiles_per_seq, d)
    row = pl.BlockSpec((tm, d), lambda i: (i, 0))
    return pl.pallas_call(
        functools.partial(_ffn_kernel, d_ff=d_ff, chunk=chunk, alpha=alpha),
        grid=(n_tok // tm,),
        in_specs=[row, mspec, mspec, mspec,
                  _resident((d, 2 * d_ff), lambda i: (0, 0)),
                  _resident((d_ff, d), lambda i: (0, 0)),
                  _resident((1, d), lambda i: (0, 0)),
                  _resident((1, d), lambda i: (0, 0))],
        out_specs=row,
        out_shape=jax.ShapeDtypeStruct((n_tok, d), F32),
        compiler_params=_params(1),
        name="ffn_sublayer",
    )(x, shift, scale, gate, w_gu, w_down, ln_g, ln_b)


def _qkv_sample_kernel(x_ref, sh_ref, sc_ref, w_ref, q_ref, k_ref, v_ref, *, mix):
    h = (x_ref[...] * (1.0 + sc_ref[...]) + sh_ref[...]).astype(BF16)
    qkv = jnp.dot(h, w_ref[...], preferred_element_type=F32)
    q_ref[...] = qkv[:, :mix]
    k_ref[...] = qkv[:, mix:2 * mix]
    v_ref[...] = qkv[:, 2 * mix:]


def _qkv_sample(x, shift, scale, w_in):
    n_tok, d = x.shape
    mix = w_in.shape[1] // 3
    full = lambda shape: pl.BlockSpec(shape, lambda i: (0,) * len(shape))
    mspec = _mod_spec(n_tok, n_tok, 1, d)
    return pl.pallas_call(
        functools.partial(_qkv_sample_kernel, mix=mix),
        grid=(1,),
        in_specs=[full((n_tok, d)), mspec, mspec, full((d, 3 * mix))],
        out_specs=[full((n_tok, mix))] * 3,
        out_shape=[jax.ShapeDtypeStruct((n_tok, mix), F32)] * 3,
        compiler_params=_params(1),
        name="qkv_projection_sample",
    )(x, shift, scale, w_in)


def _qkv_prompt_kernel(x_ref, sh_ref, sc_ref, wq_ref, wkvt_ref, *refs, mix, sb_width, inv_sqrt, n_alias):
    q32_ref, qb_ref, kt_ref, vt_ref, ktb_ref, vtb_ref, km_ref = refs[n_alias:]
    h = (x_ref[...] * (1.0 + sc_ref[...]) + sh_ref[...]).astype(BF16)
    q = jnp.dot(h, wq_ref[...], preferred_element_type=F32)
    q32_ref[...] = q[:, sb_width:]
    qb_ref[...] = (q * inv_sqrt).astype(BF16)
    kvt = _dot_t(wkvt_ref[...], h)
    kt, vt = kvt[:mix], kvt[mix:]
    kt_ref[...] = kt
    vt_ref[...] = vt
    for j in range(kt.shape[1] // MOBA_BLOCK):
        kj = kt[:, j * MOBA_BLOCK:(j + 1) * MOBA_BLOCK]
        ktb_ref[j] = kj.astype(BF16)
        vtb_ref[j] = vt[:, j * MOBA_BLOCK:(j + 1) * MOBA_BLOCK].astype(BF16)
        km_ref[:, j:j + 1] = jnp.sum(kj, axis=1, keepdims=True) * (1.0 / MOBA_BLOCK)


def _qkv_prompt(x, shift, scale, w_q, w_kv_t, kt_all, vt_all, *, layer, depth, batch, seq, tm, head_dim,
                sb_width):
    n_tok, d = x.shape
    mix = w_q.shape[1]
    tiles_per_seq = seq // tm
    blk_per_tile = tm // MOBA_BLOCK
    n_kb = seq // MOBA_BLOCK
    n_alias = 0 if kt_all is None else 2
    mspec = _mod_spec(1, tm, tiles_per_seq, d)
    row = pl.BlockSpec((tm, d), lambda i: (i, 0))
    stacked = pl.BlockSpec((None, None, mix, tm), lambda i: (layer, i // tiles_per_seq, 0, i % tiles_per_seq))
    blocked = pl.BlockSpec((None, blk_per_tile, mix, MOBA_BLOCK),
                           lambda i: (i // tiles_per_seq, i % tiles_per_seq, 0, 0))
    stacked_shape = jax.ShapeDtypeStruct((depth, batch, mix, seq), F32)
    blocked_shape = jax.ShapeDtypeStruct((batch, n_kb, mix, MOBA_BLOCK), BF16)
    in_specs = [row, mspec, mspec, _resident((d, mix), lambda i: (0, 0)),
                _resident((2 * mix, d), lambda i: (0, 0))]
    args = [x, shift, scale, w_q, w_kv_t]
    if n_alias:
        in_specs += [pl.BlockSpec(memory_space=pl.ANY)] * 2
        args += [kt_all, vt_all]
    return pl.pallas_call(
        functools.partial(_qkv_prompt_kernel, mix=mix, sb_width=sb_width, inv_sqrt=head_dim ** -0.5 * LOG2_E,
                          n_alias=n_alias),
        grid=(n_tok // tm,),
        in_specs=in_specs,
        out_specs=[pl.BlockSpec((tm, mix - sb_width), lambda i: (i, 0)),
                   pl.BlockSpec((tm, mix), lambda i: (i, 0)),
                   stacked, stacked, blocked, blocked,
                   pl.BlockSpec((None, mix, blk_per_tile), lambda i: (i, 0, 0))],
        out_shape=[jax.ShapeDtypeStruct((n_tok, mix - sb_width), F32),
                   jax.ShapeDtypeStruct((n_tok, mix), BF16),
                   stacked_shape, stacked_shape, blocked_shape, blocked_shape,
                   jax.ShapeDtypeStruct((n_tok // tm, mix, blk_per_tile), F32)],
        input_output_aliases={5: 2, 6: 3} if n_alias else {},
        compiler_params=_params(1),
        name="qkv_projection_prompt",
    )(*args)


ATTN_COLS_PER_STEP = 4


def _sb_prompt_kernel(q_ref, k_ref, v_ref, tri_ref, o_ref, *, tq, head_dim, cols):
    qi = pl.program_id(2)
    tri = tri_ref[...]
    lane_head = lax.broadcasted_iota(jnp.int32, (1, LANES), 1) // head_dim
    row = lax.broadcasted_iota(jnp.int32, (tq, tq), 0)
    col = lax.broadcasted_iota(jnp.int32, (tq, tq), 1)
    past = col < row

    n_h = LANES // head_dim
    qhs = []
    for cc in range(cols):
        q = q_ref[:, cc * LANES:(cc + 1) * LANES]
        qhs.append([jnp.where(lane_head == hh, q, jnp.zeros_like(q)) for hh in range(n_h)])

    def key_block(kj, carry, diagonal):
        cs, accs = carry
        new_cs, new_accs = [], []
        for cc in range(cols):
            kb = k_ref[kj, cc * LANES:(cc + 1) * LANES, :]
            vb = v_ref[kj, cc * LANES:(cc + 1) * LANES, :]
            pv = None
            for hh in range(n_h):
                c = cs[cc * n_h + hh]
                z = jnp.dot(qhs[cc][hh], kb, preferred_element_type=F32)
                sp = _softplus_base2(z)
                log_keep = -sp
                log_beta = z - sp
                if diagonal:
                    log_keep = jnp.where(past, log_keep, 0.0)
                suffix = jnp.dot(log_keep.astype(BF16), tri, preferred_element_type=F32) + c
                a = jnp.exp2(log_beta + suffix)
                if diagonal:
                    a = jnp.where(past, a, 0.0)
                pv_h = _dot_t(a.astype(BF16), vb)
                pv = pv_h if pv is None else jnp.where(lane_head == hh, pv_h, pv)
                new_cs.append(c + jnp.sum(log_keep, axis=1, keepdims=True))
            new_accs.append(accs[cc] + pv)
        return tuple(new_cs), tuple(new_accs)

    carry = (tuple(jnp.zeros((tq, 1), F32) for _ in range(cols * n_h)),
             tuple(jnp.zeros((tq, LANES), F32) for _ in range(cols)))
    carry = key_block(qi, carry, True)
    carry = lax.fori_loop(0, qi, lambda i, cr: key_block(qi - 1 - i, cr, False), carry)
    for cc in range(cols):
        o_ref[:, cc * LANES:(cc + 1) * LANES] = carry[1][cc]


def _attn_cols(n_cols):
    return ATTN_COLS_PER_STEP if n_cols % ATTN_COLS_PER_STEP == 0 else 1


def _kv_block_spec(n_kb, cols, col0):
    return pl.BlockSpec((None, n_kb, cols * LANES, MOBA_BLOCK), lambda b, c, i: (b, 0, col0 + c, 0))


def _sb_prompt(qb, ktb, vtb, tri, *, n_cols, head_dim):
    batch, n_kb = ktb.shape[:2]
    tq = MOBA_BLOCK
    cols = _attn_cols(n_cols)
    tile = pl.BlockSpec((tq, cols * LANES), lambda b, c, i: (b * n_kb + i, c))
    return pl.pallas_call(
        functools.partial(_sb_prompt_kernel, tq=tq, head_dim=head_dim, cols=cols),
        grid=(batch, n_cols // cols, n_kb),
        in_specs=[tile, _kv_block_spec(n_kb, cols, 0), _kv_block_spec(n_kb, cols, 0),
                  _resident((tq, tq), lambda b, c, i: (0, 0))],
        out_specs=tile,
        out_shape=jax.ShapeDtypeStruct((batch * n_kb * tq, n_cols * LANES), F32),
        compiler_params=_params(3),
        name="sb_attention_prompt",
    )(qb, ktb, vtb, tri)


def _moba_prompt_kernel(slopes_ref, q_ref, q32_ref, k_ref, v_ref, km_ref, o_ref, *,
                        tq, head_dim, n_blocks, cols):
    cb = pl.program_id(1)
    qi = pl.program_id(2)
    n_h = LANES // head_dim
    lane_head = lax.broadcasted_iota(jnp.int32, (1, LANES), 1) // head_dim
    feat_head = lax.broadcasted_iota(jnp.int32, (LANES, 1), 0) // head_dim
    blk = lax.broadcasted_iota(jnp.int32, (tq, n_blocks), 1)
    row = lax.broadcasted_iota(jnp.int32, (tq, tq), 0)
    col = lax.broadcasted_iota(jnp.int32, (tq, tq), 1)
    rel = (row - col).astype(F32)

    qhs, slopes, selected = [], [], []
    for cc in range(cols):
        q = q_ref[:, cc * LANES:(cc + 1) * LANES]
        q32 = q32_ref[:, cc * LANES:(cc + 1) * LANES]
        km = km_ref[cc * LANES:(cc + 1) * LANES, :]
        for hh in range(n_h):
            qhs.append(jnp.where(lane_head == hh, q, jnp.zeros_like(q)))
            slopes.append(slopes_ref[(cb * cols + cc) * n_h + hh])
            gate = jnp.dot(jnp.where(lane_head == hh, q32, 0.0), jnp.where(feat_head == hh, km, 0.0),
                           preferred_element_type=F32, precision=HIGHEST)
            gate = jnp.where(blk < qi, gate, -jnp.inf)
            selected.append(_top_blocks(gate, blk.astype(F32), n_blocks, qi, axis=1))

    def key_block(kj, carry, own):
        dist = rel + ((qi - kj) * tq).astype(F32)
        new = []
        for cc in range(cols):
            kb = k_ref[kj, cc * LANES:(cc + 1) * LANES, :]
            vb = v_ref[kj, cc * LANES:(cc + 1) * LANES, :]
            for hh in range(n_h):
                h = cc * n_h + hh
                m, l, acc = carry[h]
                s = jnp.dot(qhs[h], kb, preferred_element_type=F32) - slopes[h] * dist
                if own:
                    s = jnp.where(rel >= 0.0, s, NEG_BIG)
                else:
                    picked = jnp.max(jnp.where(blk == kj, selected[h], 0.0), axis=1, keepdims=True)
                    s = jnp.where(picked > 0.0, s, NEG_BIG)
                m_new = jnp.maximum(m, jnp.max(s, axis=1, keepdims=True))
                scale = jnp.exp2(m - m_new)
                p = jnp.exp2(s - m_new)
                l = scale * l + jnp.sum(p, axis=1, keepdims=True)
                acc = scale * acc + _dot_t(p.astype(BF16), vb)
                new.append((m_new, l, acc))
        return tuple(new)

    carry = tuple((jnp.full((tq, 1), NEG_BIG, F32), jnp.zeros((tq, 1), F32), jnp.zeros((tq, LANES), F32))
                  for _ in range(cols * n_h))
    carry = key_block(qi, carry, True)
    carry = lax.fori_loop(0, qi, lambda j, cr: key_block(j, cr, False), carry)
    for cc in range(cols):
        o = carry[cc * n_h][2] / carry[cc * n_h][1]
        for hh in range(1, n_h):
            o = jnp.where(lane_head == hh, carry[cc * n_h + hh][2] / carry[cc * n_h + hh][1], o)
        o_ref[:, cc * LANES:(cc + 1) * LANES] = o


def _moba_prompt(slopes, qb, q32, ktb, vtb, kmean_t, *, col0, n_cols, head_dim):
    batch, n_kb = ktb.shape[:2]
    tq = MOBA_BLOCK
    cols = _attn_cols(n_cols)
    assert col0 % cols == 0
    c0 = col0 // cols
    return pl.pallas_call(
        functools.partial(_moba_prompt_kernel, tq=tq, head_dim=head_dim, n_blocks=n_kb, cols=cols),
        grid=(batch, n_cols // cols, n_kb),
        in_specs=[pl.BlockSpec(memory_space=pltpu.SMEM),
                  pl.BlockSpec((tq, cols * LANES), lambda b, c, i: (b * n_kb + i, c0 + c)),
                  pl.BlockSpec((tq, cols * LANES), lambda b, c, i: (b * n_kb + i, c)),
                  _kv_block_spec(n_kb, cols, c0), _kv_block_spec(n_kb, cols, c0),
                  pl.BlockSpec((None, cols * LANES, n_kb), lambda b, c, i: (b, c0 + c, 0))],
        out_specs=pl.BlockSpec((tq, cols * LANES), lambda b, c, i: (b * n_kb + i, c)),
        out_shape=jax.ShapeDtypeStruct((batch * n_kb * tq, n_cols * LANES), F32),
        compiler_params=_params(3),
        name="moba_attention_prompt",
    )(slopes, qb, q32, ktb, vtb, kmean_t)


def _outproj_kernel(x_ref, osb_ref, omb_ref, gt_ref, gn_ref, w_ref, lg_ref, lb_ref, o_ref, *, alpha):
    x = x_ref[...]
    gn = gn_ref[...]
    w_sb = osb_ref.shape[1]

    def group_norm(o, g):
        return (o * lax.rsqrt(jnp.mean(o * o, axis=-1, keepdims=True) + RMS_EPS) * g).astype(BF16)

    n_sb = group_norm(osb_ref[...], gn[:, :w_sb])
    n_mb = group_norm(omb_ref[...], gn[:, w_sb:])
    mix = (jnp.dot(n_sb, w_ref[:w_sb, :], preferred_element_type=F32)
           + jnp.dot(n_mb, w_ref[w_sb:, :], preferred_element_type=F32))
    y = alpha * x + gt_ref[...] * mix
    o_ref[...] = _layer_norm(y, lg_ref[...], lb_ref[...])


def _outproj(x, o_sb, o_mb, gate, w_gn, w_out, ln_g, ln_b, *, tm, tiles_per_seq, alpha):
    n_tok, d = x.shape
    mspec = _mod_spec(gate.shape[1], tm, tiles_per_seq, d)
    row = pl.BlockSpec((tm, d), lambda i: (i, 0))
    return pl.pallas_call(
        functools.partial(_outproj_kernel, alpha=alpha),
        grid=(n_tok // tm,),
        in_specs=[row,
                  pl.BlockSpec((tm, o_sb.shape[1]), lambda i: (i, 0)),
                  pl.BlockSpec((tm, o_mb.shape[1]), lambda i: (i, 0)),
                  mspec,
                  _resident((1, w_gn.shape[1]), lambda i: (0, 0)),
                  _resident(w_out.shape, lambda i: (0, 0)),
                  _resident((1, d), lambda i: (0, 0)),
                  _resident((1, d), lambda i: (0, 0))],
        out_specs=row,
        out_shape=jax.ShapeDtypeStruct((n_tok, d), F32),
        compiler_params=_params(1),
        name="mix_outproj",
    )(x, o_sb, o_mb, gate, w_gn, w_out, ln_g, ln_b)


SAMPLE_BLOCKS_PER_STEP = 4


def _sample_attn_kernel(pt_ref, qc_ref, q_ref, kn_ref, vnt_ref, slope_ref, tri_ref, *refs,
                        n_blk, blk_per_step, pages_per_blk, n_heads, n_sb, head_dim, past_len):
    del pt_ref
    n_pg = blk_per_step * pages_per_blk
    k_refs = refs[:n_pg]
    v_refs = refs[n_pg:2 * n_pg]
    o_ref = refs[2 * n_pg]
    c_sc, osb_sc, gate_sc, m_sc, l_sc, omb_sc = refs[2 * n_pg + 1:]

    j = pl.program_id(1)
    n_mb = n_heads - n_sb
    inv_sqrt = head_dim ** -0.5

    @pl.when(j == 0)
    def _():
        c_sc[...] = jnp.zeros_like(c_sc)
        osb_sc[...] = jnp.zeros_like(osb_sc)
        gate_sc[...] = jnp.zeros_like(gate_sc)
        m_sc[...] = jnp.zeros_like(m_sc)
        l_sc[...] = jnp.zeros_like(l_sc)

    mb_r = lax.broadcasted_iota(jnp.int32, (n_mb, n_heads), 0)
    mb_l = lax.broadcasted_iota(jnp.int32, (n_mb, n_heads), 1)
    mb_slot = jnp.where(mb_l == mb_r + n_sb, 1.0, 0.0)

    def to_lanes(colv):
        return jnp.sum(mb_slot * colv, axis=0, keepdims=True)

    qc = qc_ref[...]
    slope = slope_ref[...]
    tok = lax.broadcasted_iota(jnp.int32, (1, MOBA_BLOCK), 1)
    blk_row = lax.broadcasted_iota(jnp.int32, (n_blk, n_heads), 0)
    blks = [n_blk - 1 - (j * blk_per_step + sb) for sb in range(blk_per_step)]

    zraws = [jnp.concatenate([jnp.sum(kp[...] * qc, axis=1)
                              for kp in k_refs[sb * pages_per_blk:(sb + 1) * pages_per_blk]], axis=1)
             for sb in range(blk_per_step)]

    z_sb = [zr[:n_sb] * inv_sqrt for zr in zraws]
    sps = [_softplus(z) for z in z_sb]
    log_keep = jnp.concatenate([-sp for sp in sps], axis=0)
    hi = log_keep.astype(BF16)
    lo = (log_keep - hi.astype(F32)).astype(BF16)
    sums = jnp.dot(jnp.concatenate([hi, lo], axis=0), tri_ref[...], preferred_element_type=F32)
    suffix = sums[:blk_per_step * n_sb] + sums[blk_per_step * n_sb:]
    c = c_sc[...]
    weights = []
    for sb in range(blk_per_step):
        a_sb = jnp.exp(z_sb[sb] - sps[sb] + suffix[sb * n_sb:(sb + 1) * n_sb] + c)
        c = c - jnp.sum(sps[sb], axis=1, keepdims=True)

        dist = (past_len - blks[sb] * MOBA_BLOCK - tok).astype(F32)
        s = zraws[sb][n_sb:] * inv_sqrt - slope * dist
        m_blk = jnp.max(s, axis=1, keepdims=True)
        p_mb = jnp.exp(s - m_blk)
        l_blk = jnp.sum(p_mb, axis=1, keepdims=True)
        g_blk = jnp.sum(zraws[sb][n_sb:], axis=1, keepdims=True) * (1.0 / MOBA_BLOCK)
        here = blk_row == blks[sb]
        m_sc[...] = jnp.where(here, to_lanes(m_blk), m_sc[...])
        l_sc[...] = jnp.where(here, to_lanes(l_blk), l_sc[...])
        gate_sc[...] = jnp.where(here, to_lanes(g_blk), gate_sc[...])
        weights.append(jnp.concatenate([a_sb, p_mb], axis=0).astype(BF16))
    c_sc[...] = c

    own_r = lax.broadcasted_iota(jnp.int32, (n_heads, head_dim, n_heads), 0)
    own_c = lax.broadcasted_iota(jnp.int32, (n_heads, head_dim, n_heads), 2)
    acc_step = jnp.zeros((head_dim, n_heads), F32)
    for sb in range(blk_per_step):
        vt = jnp.concatenate([vp[...].reshape(n_heads * head_dim, vp.shape[2])
                              for vp in v_refs[sb * pages_per_blk:(sb + 1) * pages_per_blk]], axis=1)
        r_all = _dot_t(vt.astype(BF16), weights[sb]).reshape(n_heads, head_dim, n_heads)
        acc = jnp.sum(jnp.where(own_r == own_c, r_all, 0.0), axis=0)
        acc_step = acc_step + acc
        omb_sc[blks[sb]] = acc
    osb_sc[...] = osb_sc[...] + acc_step

    @pl.when(j == n_blk // blk_per_step - 1)
    def _():
        blk_f = lax.broadcasted_iota(jnp.int32, (n_blk, n_heads), 0).astype(F32)
        selected = _top_blocks(gate_sc[...], blk_f, n_blk + 1, n_blk, axis=0)
        s_self = to_lanes(jnp.sum(q_ref[...] * kn_ref[...], axis=1, keepdims=True)[n_sb:]) * inv_sqrt
        m_fin = jnp.maximum(jnp.max(jnp.where(selected > 0.0, m_sc[...], NEG_BIG), axis=0, keepdims=True),
                            s_self)
        coef = jnp.where(selected > 0.0, jnp.exp(m_sc[...] - m_fin), 0.0)
        w_self = jnp.exp(s_self - m_fin)
        denom = jnp.sum(coef * l_sc[...], axis=0, keepdims=True) + w_self
        o_mb = w_self * vnt_ref[...]
        for b in range(n_blk):
            o_mb = o_mb + coef[b:b + 1, :] * omb_sc[b]
        head = lax.broadcasted_iota(jnp.int32, (1, n_heads), 1)
        o_ref[...] = jnp.where(head < n_sb, osb_sc[...], o_mb / denom)


def _sample_attention(layer, page_table, q_col, q, k_new, v_new_t, slope_col, tri, cache_kt, cache_vt, *, n_sb):
    db, n_heads, head_dim = q.shape
    page = cache_kt.shape[4]
    n_pages = page_table.shape[1]
    pages_per_blk = MOBA_BLOCK // page
    n_blk = n_pages // pages_per_blk
    past_len = n_pages * page
    n_mb = n_heads - n_sb
    blk_per_step = SAMPLE_BLOCKS_PER_STEP if n_blk % SAMPLE_BLOCKS_PER_STEP == 0 else 1
    n_pg = blk_per_step * pages_per_blk

    def page_spec(p):
        def index(b, j, pt):
            blk = n_blk - 1 - (j * blk_per_step + p // pages_per_blk)
            return (layer, pt[b * n_pages + blk * pages_per_blk + p % pages_per_blk], 0, 0, 0)
        return pl.BlockSpec((None, None, n_heads, head_dim, page), index)

    per_seq = lambda *shape: pl.BlockSpec((None,) + shape, lambda b, j, pt: (b,) + (0,) * len(shape))
    grid_spec = pltpu.PrefetchScalarGridSpec(
        num_scalar_prefetch=1,
        grid=(db, n_blk // blk_per_step),
        in_specs=[per_seq(n_heads, head_dim, 1), per_seq(n_heads, head_dim), per_seq(n_heads, head_dim),
                  per_seq(head_dim, n_heads),
                  pl.BlockSpec((n_mb, 1), lambda b, j, pt: (0, 0)),
                  pl.BlockSpec((MOBA_BLOCK, MOBA_BLOCK), lambda b, j, pt: (0, 0))]
                 + [page_spec(p) for p in range(n_pg)] * 2,
        out_specs=per_seq(head_dim, n_heads),
        scratch_shapes=[pltpu.VMEM((n_sb, 1), F32),
                        pltpu.VMEM((head_dim, n_heads), F32),
                        pltpu.VMEM((n_blk, n_heads), F32),
                        pltpu.VMEM((n_blk, n_heads), F32),
                        pltpu.VMEM((n_blk, n_heads), F32),
                        pltpu.VMEM((n_blk, head_dim, n_heads), F32)])
    return pl.pallas_call(
        functools.partial(_sample_attn_kernel, n_blk=n_blk, blk_per_step=blk_per_step,
                          pages_per_blk=pages_per_blk, n_heads=n_heads, n_sb=n_sb, head_dim=head_dim,
                          past_len=past_len),
        grid_spec=grid_spec,
        out_shape=jax.ShapeDtypeStruct((db, head_dim, n_heads), F32),
        compiler_params=_params(2),
        name="sample_paged_attention",
    )(page_table.reshape(-1), q_col, q, k_new, v_new_t, slope_col, tri,
      *([cache_kt] * n_pg), *([cache_vt] * n_pg))


def _alibi_slopes(n):
    return np.asarray(2.0 ** (-8.0 * np.arange(1, n + 1) / n), dtype=np.float32)


def kernel(x_prompt, x_sample, cache_k, cache_v, page_table, c_prompt, c_sample, w_mod, b_mod,
           ln_g, ln_b, w_ffn_gu, w_ffn_down, w_in, w_group_norm, w_out):
    batch, seq, d = x_prompt.shape
    db, dec_seq, _ = x_sample.shape
    depth, _, page, n_heads, head_dim = cache_k.shape
    mix = n_heads * head_dim
    n_sb = n_heads // 2
    n_mb = n_heads - n_sb
    sb_width = n_sb * head_dim
    alpha = (2.0 * depth) ** 0.25
    assert dec_seq == 1 and mix == d
    assert seq % MOBA_BLOCK == 0 and MOBA_BLOCK % page == 0
    assert (page_table.shape[1] * page) % MOBA_BLOCK == 0
    assert sb_width % LANES == 0 and LANES % head_dim == 0

    tm = 512 if seq % 512 == 0 else MOBA_BLOCK
    tiles_per_seq = seq // tm
    n_tok = batch * seq
    n_kb = seq // MOBA_BLOCK
    sb_cols = sb_width // LANES
    mb_cols = (mix - sb_width) // LANES

    slopes = jnp.asarray(_alibi_slopes(n_mb))
    tri_np = np.tril(np.ones((MOBA_BLOCK, MOBA_BLOCK), np.float32), -1)
    tri_bf16 = jnp.asarray(tri_np, dtype=BF16)

    w_gu_b = w_ffn_gu.astype(BF16)
    w_dn_b = w_ffn_down.astype(BF16)
    w_in_b = w_in.astype(BF16)
    w_q_b = w_in_b[:, :, :mix]
    w_kv_t_b = jnp.swapaxes(w_in_b[:, :, mix:], 1, 2)
    w_out_b = w_out.astype(BF16)

    mod = _modulation(jnp.concatenate([c_prompt, c_sample], axis=0), w_mod, b_mod)
    mod = mod.reshape(depth, batch + db, N_SUBLAYERS, 3, d)

    cache_kt = jnp.transpose(cache_k, (0, 1, 3, 4, 2))
    cache_vt = jnp.transpose(cache_v, (0, 1, 3, 4, 2))

    xp = x_prompt.reshape(n_tok, d)
    xs = x_sample.reshape(db, d)
    kt_all = vt_all = None
    sk, sv = [], []
    for layer in range(depth):
        def mods(sub, prompt):
            if prompt:
                m = mod[layer, :batch, sub]
                return tuple(m[:, i][:, None, :] for i in range(3))
            m = mod[layer, batch:, sub]
            return tuple(m[:, i][None, :, :] for i in range(3))

        def lnp(sub):
            return ln_g[layer, sub][None, :], ln_b[layer, sub][None, :]

        def ffn(x, sub, which, prompt):
            sh, sc, gt = mods(sub, prompt)
            g, b = lnp(sub)
            return _ffn(x, sh, sc, gt, w_gu_b[layer, which], w_dn_b[layer, which], g, b,
                        tm=tm if prompt else db, tiles_per_seq=tiles_per_seq, alpha=alpha)

        xp = ffn(xp, 0, 0, True)
        sh, sc, gt = mods(1, True)
        q32, qb, kt_all, vt_all, ktb, vtb, km = _qkv_prompt(
            xp, sh, sc, w_q_b[layer], w_kv_t_b[layer], kt_all, vt_all, layer=layer, depth=depth,
            batch=batch, seq=seq, tm=tm, head_dim=head_dim, sb_width=sb_width)
        kmean_t = jnp.transpose(km.reshape(batch, tiles_per_seq, mix, tm // MOBA_BLOCK),
                                (0, 2, 1, 3)).reshape(batch, mix, n_kb)
        o_sb = _sb_prompt(qb, ktb, vtb, tri_bf16, n_cols=sb_cols, head_dim=head_dim)
        o_mb = _moba_prompt(slopes * LOG2_E, qb, q32, ktb, vtb, kmean_t, col0=sb_cols, n_cols=mb_cols,
                            head_dim=head_dim)
        g, b = lnp(1)
        xp = _outproj(xp, o_sb, o_mb, gt, w_group_norm[layer][None, :], w_out_b[layer], g, b,
                      tm=tm, tiles_per_seq=tiles_per_seq, alpha=alpha)
        xp = ffn(xp, 2, 1, True)

        xs = ffn(xs, 0, 0, False)
        sh, sc, gt = mods(1, False)
        q32, k32, v32 = _qkv_sample(xs, sh, sc, w_in_b[layer])
        q3 = q32.reshape(db, n_heads, head_dim)
        k3 = k32.reshape(db, n_heads, head_dim)
        v3 = v32.reshape(db, n_heads, head_dim)
        o_t = _sample_attention(layer, page_table, q3[..., None], q3, k3, jnp.swapaxes(v3, 1, 2),
                                slopes.reshape(n_mb, 1), tri_bf16, cache_kt, cache_vt, n_sb=n_sb)
        o = jnp.swapaxes(o_t, 1, 2).reshape(db, mix)
        g, b = lnp(1)
        xs = _outproj(xs, o[:, :sb_width], o[:, sb_width:], gt, w_group_norm[layer][None, :], w_out_b[layer],
                      g, b, tm=db, tiles_per_seq=1, alpha=alpha)
        xs = ffn(xs, 2, 1, False)
        sk.append(k3.reshape(db, 1, n_heads, head_dim))
        sv.append(v3.reshape(db, 1, n_heads, head_dim))

    def token_major(t_all):
        return jnp.transpose(t_all.reshape(depth, batch, n_heads, head_dim, seq), (0, 1, 4, 2, 3))

    return (xp.reshape(batch, seq, d), xs.reshape(db, 1, d),
            token_major(kt_all), token_major(vt_all), jnp.stack(sk), jnp.stack(sv))
```
